```python
import math
import jax
import jax.numpy as jnp
from jax import lax
import numpy as np

D_MODEL = 1024
BATCH = 2
SEQ = 8192
DEPTH = 4
DEC_BATCH = 128
DEC_SEQ = 4
PAST_LEN = 2048
PAGE_SIZE = 128

MIX_WIDTH = D_MODEL
ATTN_WIDTH = MIX_WIDTH // 2
SSM_WIDTH = MIX_WIDTH - ATTN_WIDTH
HEAD_DIM = 64
N_HEADS = ATTN_WIDTH // HEAD_DIM
MOBA_BLOCK = 256
MOBA_TOPK = 3
Q_CHUNK = 64
ROPE_THETA = 10000.0
SSM_GROUP_DIM = 16
SSM_GROUPS = SSM_WIDTH // SSM_GROUP_DIM
SSM_STATE = 64
N_GROUPS = 4
EXPERTS_PER_GROUP = 4
N_EXPERTS = N_GROUPS * EXPERTS_PER_GROUP
TOPK_IN_GROUP = 2
D_EXPERT = D_MODEL // 2
ALPHA = (2.0 * DEPTH) ** 0.25
BETA = (8.0 * DEPTH) ** -0.25
LN_EPS = 1e-5
RMS_EPS = 1e-6
NEG_INF = -1e30

kernel_name = 'hymba_s5_moba_hmoe_decode_step'


def rope_tables(pos):
    inv_freq = 1.0 / jnp.power(ROPE_THETA, jnp.arange(0, HEAD_DIM, 2, dtype=jnp.float32) / HEAD_DIM)
    ang = pos.astype(jnp.float32)[:, None] * inv_freq[None, :]
    return jnp.cos(ang), jnp.sin(ang)


def apply_rope(x, cos, sin):
    xf = x.astype(jnp.float32)
    x1, x2 = xf[..., : HEAD_DIM // 2], xf[..., HEAD_DIM // 2:]
    c, s = cos[None, :, None, :], sin[None, :, None, :]
    return jnp.concatenate([x1 * c - x2 * s, x2 * c + x1 * s], axis=-1).astype(x.dtype)


def layer_norm(x, g, b):
    xf = x.astype(jnp.float32)
    mu = jnp.mean(xf, axis=-1, keepdims=True)
    var = jnp.mean(jnp.square(xf - mu), axis=-1, keepdims=True)
    return ((xf - mu) * lax.rsqrt(var + LN_EPS) * g.astype(jnp.float32) + b.astype(jnp.float32)).astype(x.dtype)


def rms_gain(y, g):
    yf = y.astype(jnp.float32)
    return (yf * lax.rsqrt(jnp.mean(jnp.square(yf), axis=-1, keepdims=True) + RMS_EPS) * g.astype(jnp.float32)).astype(y.dtype)


def moba_attention(q, k, v, q_pos, q_chunk):
    bsz, tq = q.shape[0], q.shape[1]
    tk = k.shape[1]
    nb = -(-tk // MOBA_BLOCK)
    pad = nb * MOBA_BLOCK - tk
    k = jnp.pad(k, ((0, 0), (0, pad), (0, 0), (0, 0)))
    v = jnp.pad(v, ((0, 0), (0, pad), (0, 0), (0, 0)))
    kb = k.reshape(bsz, nb, MOBA_BLOCK, N_HEADS, HEAD_DIM).transpose(0, 3, 1, 2, 4)
    vb = v.reshape(bsz, nb, MOBA_BLOCK, N_HEADS, HEAD_DIM).transpose(0, 3, 1, 2, 4)
    kmean = jnp.mean(kb.astype(jnp.float32), axis=3)
    n_sel = min(MOBA_TOPK, nb)
    n_chunks = tq // q_chunk
    qc = q.reshape(bsz, n_chunks, q_chunk, N_HEADS, HEAD_DIM).transpose(1, 0, 3, 2, 4)
    pc = q_pos.reshape(n_chunks, q_chunk)
    scale = HEAD_DIM ** -0.5
    gather = jax.vmap(jax.vmap(lambda blocks, idx: blocks[idx]))

    def one_chunk(args):
        q_c, p_c = args
        own = p_c // MOBA_BLOCK
        gate = jnp.einsum('bhqd,bhnd->bhqn', q_c.astype(jnp.float32), kmean)
        fully_past = jnp.arange(nb)[None, :] < own[:, None]
        gate = jnp.where(fully_past, gate, NEG_INF)
        _, top = lax.top_k(gate, n_sel)
        own_b = jnp.broadcast_to(own[None, None, :, None], top.shape[:3] + (1,)).astype(top.dtype)
        idx = jnp.concatenate([top, own_b], axis=-1)
        ks = gather(kb, idx)
        vs = gather(vb, idx)
        s = jnp.einsum('bhqd,bhqskd->bhqsk', q_c, ks, preferred_element_type=jnp.float32) * scale
        kpos = idx[..., None] * MOBA_BLOCK + jnp.arange(MOBA_BLOCK, dtype=idx.dtype)
        slot_ok = jnp.concatenate([jnp.arange(n_sel)[None, :] < own[:, None],
                                   jnp.ones((q_chunk, 1), dtype=bool)], axis=-1)
        ok = slot_ok[None, None, :, :, None] & (kpos <= p_c[None, None, :, None, None])
        s = jnp.where(ok, s, NEG_INF)
        p = jax.nn.softmax(s.reshape(s.shape[:3] + (-1,)), axis=-1).reshape(s.shape)
        return jnp.einsum('bhqsk,bhqskd->bhqd', p.astype(vs.dtype), vs)

    out = lax.map(one_chunk, (qc, pc))
    return out.transpose(1, 0, 3, 2, 4).reshape(bsz, tq, N_HEADS * HEAD_DIM)


def s5_mixer(u, h0, a_re, a_im, log_dt, b_re, b_im, c_re, c_im, d, w_glu, b_glu):
    f32 = jnp.float32
    bsz, seq_len, _ = u.shape
    uf = u.astype(f32)
    ug = uf.reshape(bsz, seq_len, SSM_GROUPS, SSM_GROUP_DIM)
    dt = jnp.exp(log_dt.astype(f32))[:, None]
    ar, ai = a_re.astype(f32), a_im.astype(f32)
    mag = jnp.exp(ar * dt)
    lr, li = mag * jnp.cos(ai * dt), mag * jnp.sin(ai * dt)
    den = ar * ar + ai * ai
    nr = lr - 1.0
    fr = (nr * ar + li * ai) / den
    fi = (li * ar - nr * ai) / den
    br_, bi_ = b_re.astype(f32), b_im.astype(f32)
    bbr = fr[..., None] * br_ - fi[..., None] * bi_
    bbi = fr[..., None] * bi_ + fi[..., None] * br_
    bu_re = jnp.einsum('blgc,gpc->blgp', ug, bbr)
    bu_im = jnp.einsum('blgc,gpc->blgp', ug, bbi)
    if h0 is not None:
        h0r, h0i = h0[0].astype(f32), h0[1].astype(f32)
        bu_re = bu_re.at[:, 0].add(lr * h0r - li * h0i)
        bu_im = bu_im.at[:, 0].add(lr * h0i + li * h0r)
    a_r = jnp.broadcast_to(lr, bu_re.shape)
    a_i = jnp.broadcast_to(li, bu_re.shape)

    def combine(e1, e2):
        a1r, a1i, b1r, b1i = e1
        a2r, a2i, b2r, b2i = e2
        return (a2r * a1r - a2i * a1i, a2r * a1i + a2i * a1r,
                a2r * b1r - a2i * b1i + b2r, a2r * b1i + a2i * b1r + b2i)

    _, _, hr, hi = lax.associative_scan(combine, (a_r, a_i, bu_re, bu_im), axis=1)
    y = (jnp.einsum('blgp,gcp->blgc', hr, c_re.astype(f32))
         - jnp.einsum('blgp,gcp->blgc', hi, c_im.astype(f32))).reshape(bsz, seq_len, SSM_WIDTH)
    y = y + d.astype(f32) * uf
    g = jax.nn.gelu(y)
    out = g * jax.nn.sigmoid(g @ w_glu.astype(f32) + b_glu.astype(f32))
    return out.astype(u.dtype), hr[:, -1], hi[:, -1]


def hier_moe(x, w_group, b_group, w_expert, b_expert, w_gate, w_up, w_down):
    f32 = jnp.float32
    bsz, seq_len, dm = x.shape
    xt = x.reshape(bsz * seq_len, dm)
    lg = (xt @ w_group).astype(f32) + b_group.astype(f32)
    p_group = jax.nn.softmax(lg, axis=-1)
    g_sel = jnp.argmax(lg, axis=-1)
    p_sel = jnp.take_along_axis(p_group, g_sel[:, None], axis=-1)
    le = ((xt @ w_expert).astype(f32) + b_expert.astype(f32)).reshape(-1, N_GROUPS, EXPERTS_PER_GROUP)
    le_g = jnp.take_along_axis(le, g_sel[:, None, None], axis=1)[:, 0]
    top_v, top_i = lax.top_k(le_g, TOPK_IN_GROUP)
    w_sel = p_sel * jax.nn.softmax(top_v, axis=-1)
    expert_ids = g_sel[:, None] * EXPERTS_PER_GROUP + top_i
    combine = jnp.sum(jax.nn.one_hot(expert_ids, N_EXPERTS, dtype=f32) * w_sel[..., None], axis=1)
    y = jnp.zeros(xt.shape, f32)
    for e in range(N_EXPERTS):
        h = jax.nn.silu(xt @ w_gate[e]) * (xt @ w_up[e])
        y = y + combine[:, e:e + 1] * (h @ w_down[e]).astype(f32)
    return y.astype(x.dtype).reshape(bsz, seq_len, dm)


def trunk_layer(x, pos, q_chunk, k_past, v_past, h0, lp):
    bsz, seq_len, _ = x.shape
    proj = x @ lp['w_in']
    q = proj[..., :ATTN_WIDTH].reshape(bsz, seq_len, N_HEADS, HEAD_DIM)
    k = proj[..., ATTN_WIDTH:2 * ATTN_WIDTH].reshape(bsz, seq_len, N_HEADS, HEAD_DIM)
    v = proj[..., 2 * ATTN_WIDTH:3 * ATTN_WIDTH].reshape(bsz, seq_len, N_HEADS, HEAD_DIM)
    u = proj[..., 3 * ATTN_WIDTH:]
    cos, sin = rope_tables(pos)
    q = apply_rope(q, cos, sin)
    k = apply_rope(k, cos, sin)
    if k_past is None:
        k_all, v_all = k, v
    else:
        k_all = jnp.concatenate([k_past.astype(k.dtype), k], axis=1)
        v_all = jnp.concatenate([v_past.astype(v.dtype), v], axis=1)
    attn = moba_attention(q, k_all, v_all, pos, q_chunk)
    ssm, h_re, h_im = s5_mixer(u, h0, lp['a_re'], lp['a_im'], lp['log_dt'], lp['b_re'], lp['b_im'],
                               lp['c_re'], lp['c_im'], lp['d'], lp['w_glu'], lp['b_glu'])
    mixed = jnp.concatenate([rms_gain(attn, lp['g_attn']), rms_gain(ssm, lp['g_ssm'])], axis=-1) @ lp['w_out']
    x = layer_norm(ALPHA * x + mixed, lp['ln1_g'], lp['ln1_b'])
    moe = hier_moe(x, lp['w_group'], lp['b_group'], lp['w_expert'], lp['b_expert'],
                   lp['w_gate'], lp['w_up'], lp['w_down'])
    x = layer_norm(ALPHA * x + moe, lp['ln2_g'], lp['ln2_b'])
    return x, k, v, h_re, h_im


def setup_inputs(seed: int = 0) -> dict:
    key = jax.random.key(seed)
    ks = jax.random.split(key, 40)
    f32 = jnp.float32
    n_pages = PAST_LEN // PAGE_SIZE
    n_pool = (5 * DEC_BATCH * n_pages) // 4

    def nrm(k, shape, scale):
        return jax.random.normal(k, shape, f32) * scale

    x_prompt = nrm(ks[0], (BATCH, SEQ, D_MODEL), 1.0)
    x_sample = nrm(ks[1], (DEC_BATCH, DEC_SEQ, D_MODEL), 1.0)
    cache_k = nrm(ks[2], (DEPTH, n_pool, PAGE_SIZE, N_HEADS, HEAD_DIM), 1.0)
    cache_v = nrm(ks[3], (DEPTH, n_pool, PAGE_SIZE, N_HEADS, HEAD_DIM), 1.0)
    state_ssm_re = nrm(ks[4], (DEPTH, DEC_BATCH, SSM_GROUPS, SSM_STATE), 0.3)
    state_ssm_im = nrm(ks[5], (DEPTH, DEC_BATCH, SSM_GROUPS, SSM_STATE), 0.3)
    page_table = jax.random.permutation(ks[6], n_pool)[: DEC_BATCH * n_pages].reshape(DEC_BATCH, n_pages).astype(jnp.int32)
    w_in = nrm(ks[7], (DEPTH, D_MODEL, 3 * ATTN_WIDTH + SSM_WIDTH), D_MODEL ** -0.5)
    w_out = nrm(ks[8], (DEPTH, MIX_WIDTH, D_MODEL), BETA * MIX_WIDTH ** -0.5)
    norm_attn_g = 1.0 + nrm(ks[9], (DEPTH, ATTN_WIDTH), 0.02)
    norm_ssm_g = 1.0 + nrm(ks[10], (DEPTH, SSM_WIDTH), 0.02)
    n_idx = jnp.arange(SSM_STATE, dtype=f32)
    ssm_a_re = -0.5 + nrm(ks[11], (DEPTH, SSM_GROUPS, SSM_STATE), 0.01)
    ssm_a_im = math.pi * n_idx + nrm(ks[12], (DEPTH, SSM_GROUPS, SSM_STATE), 0.01)
    ssm_log_dt = jax.random.uniform(ks[13], (DEPTH, SSM_GROUPS), f32, math.log(1e-3), math.log(1e-1))
    ssm_b_re = nrm(ks[14], (DEPTH, SSM_GROUPS, SSM_STATE, SSM_GROUP_DIM), (2 * SSM_GROUP_DIM) ** -0.5)
    ssm_b_im = nrm(ks[15], (DEPTH, SSM_GROUPS, SSM_STATE, SSM_GROUP_DIM), (2 * SSM_GROUP_DIM) ** -0.5)
    ssm_c_re = nrm(ks[16], (DEPTH, SSM_GROUPS, SSM_GROUP_DIM, SSM_STATE), SSM_STATE ** -0.5)
    ssm_c_im = nrm(ks[17], (DEPTH, SSM_GROUPS, SSM_GROUP_DIM, SSM_STATE), SSM_STATE ** -0.5)
    ssm_d = nrm(ks[18], (DEPTH, SSM_WIDTH), 1.0)
    w_glu = nrm(ks[19], (DEPTH, SSM_WIDTH, SSM_WIDTH), SSM_WIDTH ** -0.5)
    b_glu = nrm(ks[20], (DEPTH, SSM_WIDTH), 0.01)
    ln1_g = 1.0 + nrm(ks[21], (DEPTH, D_MODEL), 0.02)
    ln1_b = nrm(ks[22], (DEPTH, D_MODEL), 0.02)
    w_group = nrm(ks[23], (DEPTH, D_MODEL, N_GROUPS), D_MODEL ** -0.5)
    b_group = nrm(ks[24], (DEPTH, N_GROUPS), 0.01)
    w_expert = nrm(ks[25], (DEPTH, D_MODEL, N_EXPERTS), D_MODEL ** -0.5)
    b_expert = nrm(ks[26], (DEPTH, N_EXPERTS), 0.01)
    w_gate = nrm(ks[27], (DEPTH, N_EXPERTS, D_MODEL, D_EXPERT), D_MODEL ** -0.5)
    w_up = nrm(ks[28], (DEPTH, N_EXPERTS, D_MODEL, D_EXPERT), D_MODEL ** -0.5)
    w_down = nrm(ks[29], (DEPTH, N_EXPERTS, D_EXPERT, D_MODEL), BETA * D_EXPERT ** -0.5)
    ln2_g = 1.0 + nrm(ks[30], (DEPTH, D_MODEL), 0.02)
    ln2_b = nrm(ks[31], (DEPTH, D_MODEL), 0.02)
    return {'x_prompt': x_prompt, 'x_sample': x_sample, 'cache_k': cache_k, 'cache_v': cache_v,
            'state_ssm_re': state_ssm_re, 'state_ssm_im': state_ssm_im, 'page_table': page_table,
            'w_in': w_in, 'w_out': w_out, 'norm_attn_g': norm_attn_g, 'norm_ssm_g': norm_ssm_g,
            'ssm_a_re': ssm_a_re, 'ssm_a_im': ssm_a_im, 'ssm_log_dt': ssm_log_dt,
            'ssm_b_re': ssm_b_re, 'ssm_b_im': ssm_b_im, 'ssm_c_re': ssm_c_re, 'ssm_c_im': ssm_c_im,
            'ssm_d': ssm_d, 'w_glu': w_glu, 'b_glu': b_glu, 'ln1_g': ln1_g, 'ln1_b': ln1_b,
            'w_group': w_group, 'b_group': b_group, 'w_expert': w_expert, 'b_expert': b_expert,
            'w_gate': w_gate, 'w_up': w_up, 'w_down': w_down, 'ln2_g': ln2_g, 'ln2_b': ln2_b}


def reference(x_prompt, x_sample, cache_k, cache_v, state_ssm_re, state_ssm_im, page_table,
              w_in, w_out, norm_attn_g, norm_ssm_g, ssm_a_re, ssm_a_im, ssm_log_dt,
              ssm_b_re, ssm_b_im, ssm_c_re, ssm_c_im, ssm_d, w_glu, b_glu, ln1_g, ln1_b,
              w_group, b_group, w_expert, b_expert, w_gate, w_up, w_down, ln2_g, ln2_b):
    dec_batch, n_pages = page_table.shape
    past_len = n_pages * PAGE_SIZE
    seq_len = x_prompt.shape[1]
    dec_seq = x_sample.shape[1]
    pos_prompt = jnp.arange(seq_len, dtype=jnp.int32)
    pos_sample = past_len + jnp.arange(dec_seq, dtype=jnp.int32)
    q_chunk_prompt = min(Q_CHUNK, seq_len)
    xp, xs = x_prompt, x_sample
    kp_l, vp_l, hrp_l, hip_l = [], [], [], []
    ks_l, vs_l, hrs_l, his_l = [], [], [], []
    for l in range(DEPTH):
        lp = dict(w_in=w_in[l], w_out=w_out[l], g_attn=norm_attn_g[l], g_ssm=norm_ssm_g[l],
                  a_re=ssm_a_re[l], a_im=ssm_a_im[l], log_dt=ssm_log_dt[l],
                  b_re=ssm_b_re[l], b_im=ssm_b_im[l], c_re=ssm_c_re[l], c_im=ssm_c_im[l],
                  d=ssm_d[l], w_glu=w_glu[l], b_glu=b_glu[l], ln1_g=ln1_g[l], ln1_b=ln1_b[l],
                  w_group=w_group[l], b_group=b_group[l], w_expert=w_expert[l], b_expert=b_expert[l],
                  w_gate=w_gate[l], w_up=w_up[l], w_down=w_down[l], ln2_g=ln2_g[l], ln2_b=ln2_b[l])
        xp, kp, vp, hrp, hip = trunk_layer(xp, pos_prompt, q_chunk_prompt, None, None, None, lp)
        kp_l.append(kp); vp_l.append(vp); hrp_l.append(hrp); hip_l.append(hip)
        k_past = cache_k[l, page_table].reshape(dec_batch, past_len, N_HEADS, HEAD_DIM)
        v_past = cache_v[l, page_table].reshape(dec_batch, past_len, N_HEADS, HEAD_DIM)
        xs, kn, vn, hrs, his = trunk_layer(xs, pos_sample, 1, k_past, v_past,
                                           (state_ssm_re[l], state_ssm_im[l]), lp)
        ks_l.append(kn); vs_l.append(vn); hrs_l.append(hrs); his_l.append(his)
    return (xp, xs, jnp.stack(kp_l), jnp.stack(vp_l), jnp.stack(hrp_l), jnp.stack(hip_l),
            jnp.stack(ks_l), jnp.stack(vs_l), jnp.stack(hrs_l), jnp.stack(his_l))
```

```python
import functools
import math

import jax
import jax.numpy as jnp
from jax import lax
from jax.experimental import pallas as pl
from jax.experimental.pallas import tpu as pltpu

F32 = jnp.float32
BF16 = jnp.bfloat16
HIGHEST = lax.Precision.HIGHEST

D_MODEL = 1024
ATTN_WIDTH = 512
SSM_WIDTH = 512
HEAD_DIM = 64
N_HEADS = 8
MOBA_BLOCK = 256
MOBA_TOPK = 3
PAGE_SIZE = 128
ROPE_THETA = 10000.0
SSM_GROUP_DIM = 16
SSM_GROUPS = 32
SSM_STATE = 64
N_GROUPS = 4
EXPERTS_PER_GROUP = 4
N_EXPERTS = 16
D_EXPERT = 512
DEPTH = 4
ALPHA = (2.0 * DEPTH) ** 0.25
LN_EPS = 1e-5
RMS_EPS = 1e-6
NEG_INF = -1e30
PROJ_WIDTH = 3 * ATTN_WIDTH + SSM_WIDTH

LANES = 128
SSM_CHUNK = 64
VMEM_LIMIT = 48 * 1024 * 1024


def _params(*sem):
    return pltpu.CompilerParams(dimension_semantics=sem, vmem_limit_bytes=VMEM_LIMIT)


def _rope_tables(pos):
    inv_freq = 1.0 / jnp.power(ROPE_THETA, jnp.arange(0, HEAD_DIM, 2, dtype=F32) / HEAD_DIM)
    ang = pos.astype(F32)[:, None] * inv_freq[None, :]
    cos, sin = jnp.cos(ang), jnp.sin(ang)
    cos_t = jnp.concatenate([cos, cos, cos, cos], axis=-1)
    sin_t = jnp.concatenate([-sin, sin, -sin, sin], axis=-1)
    return cos_t, sin_t


def _rope(x, cos, sin):
    tm = x.shape[0]
    lane = lax.broadcasted_iota(jnp.int32, (tm, LANES), 1)
    first_half = (lane % HEAD_DIM) < (HEAD_DIM // 2)
    outs = []
    for c in range(x.shape[1] // LANES):
        xc = x[:, c * LANES:(c + 1) * LANES]
        partner = jnp.where(first_half, pltpu.roll(xc, LANES - HEAD_DIM // 2, 1),
                            pltpu.roll(xc, HEAD_DIM // 2, 1))
        outs.append(xc * cos + partner * sin)
    return jnp.concatenate(outs, axis=1)


def _inproj_prompt_kernel(x_ref, w_ref, cos_ref, sin_ref,
                          k_ref, v_ref, u_ref, qh_ref, kh_ref, vt_ref, km_ref):
    tm = x_ref.shape[0]
    proj = jnp.dot(x_ref[...].astype(BF16), w_ref[...], preferred_element_type=F32)
    cos, sin = cos_ref[...], sin_ref[...]
    q = _rope(proj[:, :ATTN_WIDTH], cos, sin)
    k = _rope(proj[:, ATTN_WIDTH:2 * ATTN_WIDTH], cos, sin)
    v = proj[:, 2 * ATTN_WIDTH:3 * ATTN_WIDTH]
    k_ref[...] = k
    v_ref[...] = v
    u_ref[...] = proj[:, 3 * ATTN_WIDTH:]
    scale = HEAD_DIM ** -0.5
    for h in range(N_HEADS):
        qh_ref[h] = q[:, h * HEAD_DIM:(h + 1) * HEAD_DIM] * scale
        kh_ref[h] = k[:, h * HEAD_DIM:(h + 1) * HEAD_DIM].astype(BF16)
    vt = v.T
    for blk in range(tm // MOBA_BLOCK):
        sl = slice(blk * MOBA_BLOCK, (blk + 1) * MOBA_BLOCK)
        vt_ref[:, blk] = vt[:, sl].reshape(N_HEADS, HEAD_DIM, MOBA_BLOCK).astype(BF16)
        km_ref[blk] = jnp.mean(k[sl], axis=0, keepdims=True)


def _inproj_prompt(x, w_in, cos_t, sin_t, seq_len, tm=512):
    t = x.shape[0]
    n_tiles_seq = seq_len // tm
    nblk = t // MOBA_BLOCK
    bpt = tm // MOBA_BLOCK
    row = lambda i: (i, 0)
    out_shape = (
        jax.ShapeDtypeStruct((t, ATTN_WIDTH), F32),
        jax.ShapeDtypeStruct((t, ATTN_WIDTH), F32),
        jax.ShapeDtypeStruct((t, SSM_WIDTH), F32),
        jax.ShapeDtypeStruct((N_HEADS, t, HEAD_DIM), F32),
        jax.ShapeDtypeStruct((N_HEADS, t, HEAD_DIM), BF16),
        jax.ShapeDtypeStruct((N_HEADS, nblk, HEAD_DIM, MOBA_BLOCK), BF16),
        jax.ShapeDtypeStruct((nblk, 1, ATTN_WIDTH), F32),
    )
    return pl.pallas_call(
        _inproj_prompt_kernel,
        grid=(t // tm,),
        in_specs=[
            pl.BlockSpec((tm, D_MODEL), row),
            pl.BlockSpec((D_MODEL, PROJ_WIDTH), lambda i: (0, 0)),
            pl.BlockSpec((tm, LANES), lambda i: (i % n_tiles_seq, 0)),
            pl.BlockSpec((tm, LANES), lambda i: (i % n_tiles_seq, 0)),
        ],
        out_specs=(
            pl.BlockSpec((tm, ATTN_WIDTH), row),
            pl.BlockSpec((tm, ATTN_WIDTH), row),
            pl.BlockSpec((tm, SSM_WIDTH), row),
            pl.BlockSpec((N_HEADS, tm, HEAD_DIM), lambda i: (0, i, 0)),
            pl.BlockSpec((N_HEADS, tm, HEAD_DIM), lambda i: (0, i, 0)),
            pl.BlockSpec((N_HEADS, bpt, HEAD_DIM, MOBA_BLOCK), lambda i: (0, i, 0, 0)),
            pl.BlockSpec((bpt, 1, ATTN_WIDTH), lambda i: (i, 0, 0)),
        ),
        out_shape=out_shape,
        compiler_params=_params("parallel"),
        name="inproj_prompt",
    )(x, w_in, cos_t, sin_t)


def _inproj_sample_kernel(x_ref, w_ref, cos_ref, sin_ref, q_ref, k_ref, v_ref, u_ref):
    proj = jnp.dot(x_ref[...].astype(BF16), w_ref[...], preferred_element_type=F32)
    cos, sin = cos_ref[...], sin_ref[...]
    q_ref[...] = _rope(proj[:, :ATTN_WIDTH], cos, sin) * (HEAD_DIM ** -0.5)
    k_ref[...] = _rope(proj[:, ATTN_WIDTH:2 * ATTN_WIDTH], cos, sin)
    v_ref[...] = proj[:, 2 * ATTN_WIDTH:3 * ATTN_WIDTH]
    u_ref[...] = proj[:, 3 * ATTN_WIDTH:]


def _inproj_sample(x, w_in, cos_t, sin_t, tm=256):
    t = x.shape[0]
    tm = min(tm, t)
    row = lambda i: (i, 0)
    sds = jax.ShapeDtypeStruct((t, ATTN_WIDTH), F32)
    return pl.pallas_call(
        _inproj_sample_kernel,
        grid=(t // tm,),
        in_specs=[
            pl.BlockSpec((tm, D_MODEL), row),
            pl.BlockSpec((D_MODEL, PROJ_WIDTH), lambda i: (0, 0)),
            pl.BlockSpec((tm, LANES), row),
            pl.BlockSpec((tm, LANES), row),
        ],
        out_specs=tuple(pl.BlockSpec((tm, ATTN_WIDTH), row) for _ in range(4)),
        out_shape=(sds, sds, sds, sds),
        compiler_params=_params("parallel"),
        name="inproj_sample",
    )(x, w_in, cos_t, sin_t)


def _select_blocks(gate, n_valid, axis):
    nb = gate.shape[axis]
    blk = lax.broadcasted_iota(jnp.int32, gate.shape, axis)
    gate = jnp.where(blk < n_valid, gate, NEG_INF)
    sel = jnp.zeros(gate.shape, dtype=jnp.bool_)
    for r in range(MOBA_TOPK):
        top = jnp.max(gate, axis=axis, keepdims=True)
        first = jnp.min(jnp.where(gate == top, blk, nb), axis=axis, keepdims=True)
        hit = blk == first
        sel = jnp.logical_or(sel, jnp.logical_and(hit, r < n_valid))
        gate = jnp.where(hit, -jnp.inf, gate)
    return jnp.where(sel, 0.0, NEG_INF).astype(F32)


def _moba_prompt_kernel(q_ref, k_ref, vt_ref, km_ref, o_ref, bias_ref):
    qi = pl.program_id(2)
    tq = q_ref.shape[1]
    q = q_ref[0]
    gate = lax.dot_general(km_ref[0, 0], q, (((1,), (1,)), ((), ())),
                           precision=HIGHEST, preferred_element_type=F32)
    bias_ref[...] = _select_blocks(gate, qi, axis=0)
    qb = q.astype(BF16)

    def scores(j):
        return lax.dot_general(k_ref[0, j], qb, (((1,), (1,)), ((), ())),
                               preferred_element_type=F32)

    s = scores(qi)
    key_i = lax.broadcasted_iota(jnp.int32, s.shape, 0)
    qry_i = lax.broadcasted_iota(jnp.int32, s.shape, 1)
    s = jnp.where(key_i <= qry_i, s, NEG_INF)
    m0 = jnp.max(s, axis=0, keepdims=True)
    p = jnp.exp(s - m0)
    l0 = jnp.sum(p, axis=0, keepdims=True)
    acc0 = jnp.dot(vt_ref[0, qi], p.astype(BF16), preferred_element_type=F32)

    def body(j, carry):
        m, l, acc = carry
        s = scores(j) + bias_ref[pl.ds(j, 1), :]
        m_new = jnp.maximum(m, jnp.max(s, axis=0, keepdims=True))
        a = jnp.exp(m - m_new)
        p = jnp.exp(s - m_new)
        l = a * l + jnp.sum(p, axis=0, keepdims=True)
        acc = a * acc + jnp.dot(vt_ref[0, j], p.astype(BF16), preferred_element_type=F32)
        return m_new, l, acc

    _, l, acc = lax.fori_loop(0, qi, body, (m0, l0, acc0))
    o_ref[...] = acc / l


def _moba_prompt(qh, kh, vt, kmean_h, batch, seq_len):
    t = qh.shape[1]
    nb = seq_len // MOBA_BLOCK
    tq = MOBA_BLOCK
    return pl.pallas_call(
        _moba_prompt_kernel,
        grid=(batch, N_HEADS, nb),
        in_specs=[
            pl.BlockSpec((1, tq, HEAD_DIM), lambda b, h, i: (h, b * nb + i, 0)),
            pl.BlockSpec((1, nb, MOBA_BLOCK, HEAD_DIM), lambda b, h, i: (h, b, 0, 0)),
            pl.BlockSpec((1, nb, HEAD_DIM, MOBA_BLOCK), lambda b, h, i: (h, b, 0, 0)),
            pl.BlockSpec((1, 1, nb, HEAD_DIM), lambda b, h, i: (b, h, 0, 0)),
        ],
        out_specs=pl.BlockSpec((HEAD_DIM, tq), lambda b, h, i: (h, b * nb + i)),
        out_shape=jax.ShapeDtypeStruct((ATTN_WIDTH, t), F32),
        scratch_shapes=[pltpu.VMEM((nb, tq), F32)],
        compiler_params=_params("parallel", "parallel", "arbitrary"),
        name="moba_prompt",
    )(qh, kh, vt, kmean_h)


def _moba_sample_kernel(n_pages, pt_ref, q_ref, kn_ref, vn_ref, *rest):
    del pt_ref
    k_pages = rest[:n_pages]
    v_pages = rest[n_pages:2 * n_pages]
    o_ref = rest[2 * n_pages]
    nq = q_ref.shape[1]
    rows = N_HEADS * nq
    ppb = MOBA_BLOCK // PAGE_SIZE
    nb_past = n_pages // ppb

    q = q_ref[0]
    q_rep = jnp.concatenate([q] * N_HEADS, axis=0)
    r_i = lax.broadcasted_iota(jnp.int32, (rows, ATTN_WIDTH), 0)
    c_i = lax.broadcasted_iota(jnp.int32, (rows, ATTN_WIDTH), 1)
    head_mask = (c_i // HEAD_DIM) == (r_i // nq)
    qmat = jnp.where(head_mask, q_rep, 0.0)
    qmat_b = qmat.astype(BF16)

    kmeans = []
    for n in range(nb_past):
        tot = jnp.zeros((1, ATTN_WIDTH), F32)
        for pg in range(ppb):
            tot = tot + jnp.sum(k_pages[n * ppb + pg][0], axis=0, keepdims=True)
        kmeans.append(tot * (1.0 / MOBA_BLOCK))
    kmean = jnp.concatenate(kmeans, axis=0)
    gate = lax.dot_general(qmat, kmean, (((1,), (1,)), ((), ())),
                           precision=HIGHEST, preferred_element_type=F32)
    bias_t = _select_blocks(gate, nb_past, axis=1)

    s_parts = []
    for pg in range(n_pages):
        kp = k_pages[pg][0].astype(BF16)
        sp = lax.dot_general(qmat_b, kp, (((1,), (1,)), ((), ())), preferred_element_type=F32)
        n = pg // ppb
        s_parts.append(sp + bias_t[:, n:n + 1])
    sn = lax.dot_general(qmat_b, kn_ref[0].astype(BF16), (((1,), (1,)), ((), ())),
                         preferred_element_type=F32)
    key_i = lax.broadcasted_iota(jnp.int32, (rows, nq), 1)
    qry_i = lax.broadcasted_iota(jnp.int32, (rows, nq), 0) % nq
    sn = jnp.where(key_i <= qry_i, sn, NEG_INF)

    m = jnp.max(sn, axis=1, keepdims=True)
    for sp in s_parts:
        m = jnp.maximum(m, jnp.max(sp, axis=1, keepdims=True))
    pn = jnp.exp(sn - m)
    l = jnp.sum(pn, axis=1, keepdims=True)
    acc = jnp.dot(pn.astype(BF16), vn_ref[0].astype(BF16), preferred_element_type=F32)
    for pg in range(n_pages):
        p = jnp.exp(s_parts[pg] - m)
        l = l + jnp.sum(p, axis=1, keepdims=True)
        acc = acc + jnp.dot(p.astype(BF16), v_pages[pg][0].astype(BF16), preferred_element_type=F32)
    out = jnp.where(head_mask, acc / l, 0.0)
    res = out[0:nq]
    for h in range(1, N_HEADS):
        res = res + out[h * nq:(h + 1) * nq]
    o_ref[0] = res


def _moba_sample(q, k_new, v_new, cache_k, cache_v, page_table, layer, n_pool):
    bsz, nq, _ = q.shape
    n_pages = page_table.shape[1]
    base = layer * n_pool
    tok = pl.BlockSpec((1, nq, ATTN_WIDTH), lambda b, pt: (b, 0, 0))

    def page_spec(pg):
        return pl.BlockSpec((1, PAGE_SIZE, ATTN_WIDTH), lambda b, pt: (base + pt[b, pg], 0, 0))

    grid_spec = pltpu.PrefetchScalarGridSpec(
        num_scalar_prefetch=1,
        grid=(bsz,),
        in_specs=[tok, tok, tok] + [page_spec(pg) for pg in range(n_pages)] * 2,
        out_specs=tok,
    )
    return pl.pallas_call(
        functools.partial(_moba_sample_kernel, n_pages),
        grid_spec=grid_spec,
        out_shape=jax.ShapeDtypeStruct((bsz, nq, ATTN_WIDTH), F32),
        compiler_params=_params("parallel"),
        name="moba_sample",
    )(page_table, q, k_new, v_new, *([cache_k] * n_pages), *([cache_v] * n_pages))


def _s5_discretise(a_re, a_im, log_dt, b_re, b_im):
    dt = jnp.exp(log_dt.astype(F32))[:, None]
    ar, ai = a_re.astype(F32), a_im.astype(F32)
    mag = jnp.exp(ar * dt)
    lr, li = mag * jnp.cos(ai * dt), mag * jnp.sin(ai * dt)
    den = ar * ar + ai * ai
    nr = lr - 1.0
    fr = (nr * ar + li * ai) / den
    fi = (li * ar - nr * ai) / den
    br_, bi_ = b_re.astype(F32), b_im.astype(F32)
    bbr = fr[..., None] * br_ - fi[..., None] * bi_
    bbi = fr[..., None] * bi_ + fi[..., None] * br_
    return ar * dt, ai * dt, lr, li, bbr, bbi


def _s5_prompt_mats(a_re, a_im, log_dt, b_re, b_im, c_re, c_im):
    c = SSM_CHUNK
    g_, p_, cdim = SSM_GROUPS, SSM_STATE, SSM_GROUP_DIM
    adr, adi, _, _, bbr, bbi = _s5_discretise(a_re, a_im, log_dt, b_re, b_im)
    tau = jnp.arange(c + 1, dtype=F32)[:, None, None]
    pmag = jnp.exp(adr[None] * tau)
    pw_r, pw_i = pmag * jnp.cos(adi[None] * tau), pmag * jnp.sin(adi[None] * tau)
    cr, ci = c_re.astype(F32), c_im.astype(F32)
    lb_r = pw_r[..., None] * bbr[None] - pw_i[..., None] * bbi[None]
    lb_i = pw_r[..., None] * bbi[None] + pw_i[..., None] * bbr[None]
    kern = (jnp.einsum('gop,tgpi->tgoi', cr, lb_r[:c], precision=HIGHEST)
            - jnp.einsum('gop,tgpi->tgoi', ci, lb_i[:c], precision=HIGHEST))
    s_i = jnp.arange(c)[:, None]
    t_i = jnp.arange(c)[None, :]
    lag = t_i - s_i
    toep = jnp.where((lag >= 0)[:, :, None, None, None], kern[jnp.clip(lag, 0, c - 1)], 0.0)
    toep = toep.transpose(2, 0, 4, 1, 3).reshape(g_, c * cdim, c * cdim)
    w_r = lb_r[:c][::-1].transpose(1, 0, 3, 2).reshape(g_, c * cdim, p_)
    w_i = lb_i[:c][::-1].transpose(1, 0, 3, 2).reshape(g_, c * cdim, p_)
    pr, pi = pw_r[1:], pw_i[1:]
    v_r = cr[None] * pr[:, :, None, :] - ci[None] * pi[:, :, None, :]
    v_i = -cr[None] * pi[:, :, None, :] - ci[None] * pr[:, :, None, :]
    v_r = v_r.transpose(1, 3, 0, 2).reshape(g_, p_, c * cdim)
    v_i = v_i.transpose(1, 3, 0, 2).reshape(g_, p_, c * cdim)
    decay = jnp.stack([pw_r[c], pw_i[c]], axis=1)
    return toep.astype(BF16), w_r.astype(BF16), w_i.astype(BF16), v_r.astype(BF16), v_i.astype(BF16), decay


def _s5_sample_mats(a_re, a_im, log_dt, b_re, b_im, c_re, c_im):
    g_, p_, cdim = SSM_GROUPS, SSM_STATE, SSM_GROUP_DIM
    _, _, lr, li, bbr, bbi = _s5_discretise(a_re, a_im, log_dt, b_re, b_im)
    eye = jnp.eye(g_, dtype=F32)
    b_bd_r = (eye[:, None, :, None] * bbr.transpose(0, 2, 1)[:, :, None, :]).reshape(g_ * cdim, g_ * p_)
    b_bd_i = (eye[:, None, :, None] * bbi.transpose(0, 2, 1)[:, :, None, :]).reshape(g_ * cdim, g_ * p_)
    cr, ci = c_re.astype(F32), c_im.astype(F32)
    c_bd_r = (eye[:, None, :, None] * cr.transpose(0, 2, 1)[:, :, None, :]).reshape(g_ * p_, g_ * cdim)
    c_bd_i = (eye[:, None, :, None] * ci.transpose(0, 2, 1)[:, :, None, :]).reshape(g_ * p_, g_ * cdim)
    return lr.reshape(1, g_ * p_), li.reshape(1, g_ * p_), b_bd_r, b_bd_i, c_bd_r, c_bd_i


def _s5_prompt_kernel(n_batch, u_ref, toep_ref, wr_ref, wi_ref, vr_ref, vi_ref, dec_ref,
                      y_ref, hr_ref, hi_ref, sr_s, si_s, pr_s, pi_s):
    u = u_ref[0]
    n_chunks = u.shape[0]
    cpb = n_chunks // n_batch
    sr_s[...] = jnp.dot(u, wr_ref[0], preferred_element_type=F32)
    si_s[...] = jnp.dot(u, wi_ref[0], preferred_element_type=F32)
    dr = dec_ref[0, 0:1, :]
    di = dec_ref[0, 1:2, :]

    def step(kk, carry):
        new = []
        for b in range(n_batch):
            hr, hi = carry[2 * b], carry[2 * b + 1]
            r = b * cpb + kk
            pr_s[pl.ds(r, 1), :] = hr
            pi_s[pl.ds(r, 1), :] = hi
            new.append(dr * hr - di * hi + sr_s[pl.ds(r, 1), :])
            new.append(dr * hi + di * hr + si_s[pl.ds(r, 1), :])
        return tuple(new)

    zero = jnp.zeros((1, SSM_STATE), F32)
    fin = lax.fori_loop(0, cpb, step, (zero,) * (2 * n_batch))
    for b in range(n_batch):
        hr_ref[0, b:b + 1, :] = fin[2 * b]
        hi_ref[0, b:b + 1, :] = fin[2 * b + 1]
    y = jnp.dot(u, toep_ref[0], preferred_element_type=F32)
    y = y + jnp.dot(pr_s[...].astype(BF16), vr_ref[0], preferred_element_type=F32)
    y = y + jnp.dot(pi_s[...].astype(BF16), vi_ref[0], preferred_element_type=F32)
    y_ref[0] = y


def _s5_prompt(u_g, mats, n_batch):
    toep, w_r, w_i, v_r, v_i, decay = mats
    g_, n_chunks, cw = u_g.shape
    per_g = lambda *shape: pl.BlockSpec((1,) + shape, lambda g: (g,) + (0,) * len(shape))
    st = jax.ShapeDtypeStruct((g_, n_batch, SSM_STATE), F32)
    return pl.pallas_call(
        functools.partial(_s5_prompt_kernel, n_batch),
        grid=(g_,),
        in_specs=[per_g(n_chunks, cw), per_g(cw, cw), per_g(cw, SSM_STATE), per_g(cw, SSM_STATE),
                  per_g(SSM_STATE, cw), per_g(SSM_STATE, cw), per_g(2, SSM_STATE)],
        out_specs=(per_g(n_chunks, cw), per_g(n_batch, SSM_STATE), per_g(n_batch, SSM_STATE)),
        out_shape=(jax.ShapeDtypeStruct((g_, n_chunks, cw), F32), st, st),
        scratch_shapes=[pltpu.VMEM((n_chunks, SSM_STATE), F32) for _ in range(4)],
        compiler_params=_params("parallel"),
        name="s5_prompt",
    )(u_g, toep, w_r, w_i, v_r, v_i, decay)


def _s5_sample_kernel(u_ref, h0r_ref, h0i_ref, lr_ref, li_ref, br_ref, bi_ref, cr_ref, ci_ref,
                      y_ref, hr_ref, hi_ref):
    n_steps = u_ref.shape[0]
    lr, li = lr_ref[...], li_ref[...]
    hr, hi = h0r_ref[...], h0i_ref[...]
    for t in range(n_steps):
        u = u_ref[t]
        bur = jnp.dot(u, br_ref[...], precision=HIGHEST, preferred_element_type=F32)
        bui = jnp.dot(u, bi_ref[...], precision=HIGHEST, preferred_element_type=F32)
        hr, hi = lr * hr - li * hi + bur, lr * hi + li * hr + bui
        y_ref[t] = (jnp.dot(hr, cr_ref[...], precision=HIGHEST, preferred_element_type=F32)
                    - jnp.dot(hi, ci_ref[...], precision=HIGHEST, preferred_element_type=F32))
    hr_ref[...] = hr
    hi_ref[...] = hi


def _s5_sample(u_t, h0r, h0i, mats):
    n_steps, bsz, _ = u_t.shape
    st = jax.ShapeDtypeStruct(h0r.shape, F32)
    return pl.pallas_call(
        _s5_sample_kernel,
        out_shape=(jax.ShapeDtypeStruct((n_steps, bsz, SSM_WIDTH), F32), st, st),
        compiler_params=pltpu.CompilerParams(vmem_limit_bytes=VMEM_LIMIT),
        name="s5_sample",
    )(u_t, h0r, h0i, *mats)


def _layer_norm(x, g, b):
    mu = jnp.mean(x, axis=-1, keepdims=True)
    xc = x - mu
    var = jnp.mean(xc * xc, axis=-1, keepdims=True)
    return xc * lax.rsqrt(var + LN_EPS) * g + b


def _rms_gain(y, g):
    return y * lax.rsqrt(jnp.mean(y * y, axis=-1, keepdims=True) + RMS_EPS) * g


def _sigmoid(x):
    return 1.0 / (1.0 + jnp.exp(-x))


def _gelu_tanh(x):
    return 0.5 * x * (1.0 + jnp.tanh(math.sqrt(2.0 / math.pi) * (x + 0.044715 * (x * x * x))))


def _route(lg, le):
    tm = lg.shape[0]
    gi = lax.broadcasted_iota(jnp.int32, lg.shape, 1)
    gmax = jnp.max(lg, axis=-1, keepdims=True)
    g_sel = jnp.min(jnp.where(lg == gmax, gi, N_GROUPS), axis=-1, keepdims=True)
    p_sel = 1.0 / jnp.sum(jnp.exp(lg - gmax), axis=-1, keepdims=True)
    ei = lax.broadcasted_iota(jnp.int32, (tm, N_EXPERTS), 1)
    cand = jnp.where((ei // EXPERTS_PER_GROUP) == g_sel, le, -jnp.inf)
    v0 = jnp.max(cand, axis=-1, keepdims=True)
    i0 = jnp.min(jnp.where(cand == v0, ei, N_EXPERTS), axis=-1, keepdims=True)
    cand = jnp.where(ei == i0, -jnp.inf, cand)
    v1 = jnp.max(cand, axis=-1, keepdims=True)
    i1 = jnp.min(jnp.where(cand == v1, ei, N_EXPERTS), axis=-1, keepdims=True)
    e1 = jnp.exp(v1 - v0)
    w0 = p_sel / (1.0 + e1)
    w1 = p_sel * e1 / (1.0 + e1)
    return jnp.where(ei == i0, w0, 0.0) + jnp.where(ei == i1, w1, 0.0)


def _post_mixer_kernel(y_ref, u_ref, a_ref, x_ref, d_ref, wglu_ref, bglu_ref, ga_ref, gs_ref,
                       wout_ref, g1_ref, b1_ref, wr_ref, br_ref, x1_ref, comb_ref):
    y = y_ref[...] + d_ref[...] * u_ref[...]
    g = _gelu_tanh(y)
    z = g * _sigmoid(jnp.dot(g.astype(BF16), wglu_ref[...], preferred_element_type=F32) + bglu_ref[...])
    mix_in = jnp.concatenate([_rms_gain(a_ref[...], ga_ref[...]), _rms_gain(z, gs_ref[...])], axis=-1)
    mixed = jnp.dot(mix_in.astype(BF16), wout_ref[...], preferred_element_type=F32)
    x1 = _layer_norm(ALPHA * x_ref[...] + mixed, g1_ref[...], b1_ref[...])
    x1_ref[...] = x1
    logits = jnp.dot(x1, wr_ref[...], precision=HIGHEST, preferred_element_type=F32) + br_ref[...]
    comb_ref[...] = _route(logits[:, :N_GROUPS], logits[:, N_GROUPS:N_GROUPS + N_EXPERTS])


def _post_mixer(y, u, attn, x, lp, tm):
    t = x.shape[0]
    tm = min(tm, t)
    row = lambda i: (i, 0)
    full = lambda a: pl.BlockSpec(a.shape, lambda i: (0, 0))
    weights = [lp['d'], lp['w_glu'], lp['b_glu'], lp['g_attn'], lp['g_ssm'], lp['w_out'],
               lp['ln1_g'], lp['ln1_b'], lp['w_router'], lp['b_router']]
    return pl.pallas_call(
        _post_mixer_kernel,
        grid=(t // tm,),
        in_specs=[pl.BlockSpec((tm, SSM_WIDTH), row), pl.BlockSpec((tm, SSM_WIDTH), row),
                  pl.BlockSpec((tm, ATTN_WIDTH), row), pl.BlockSpec((tm, D_MODEL), row)]
                 + [full(w) for w in weights],
        out_specs=(pl.BlockSpec((tm, D_MODEL), row), pl.BlockSpec((tm, N_EXPERTS), row)),
        out_shape=(jax.ShapeDtypeStruct((t, D_MODEL), F32), jax.ShapeDtypeStruct((t, N_EXPERTS), F32)),
        compiler_params=_params("parallel"),
        name="post_mixer",
    )(y, u, attn, x, *weights)


def _moe_kernel(x_ref, comb_ref, wg_ref, wu_ref, wd_ref, g2_ref, b2_ref, o_ref, acc_ref):
    e = pl.program_id(1)

    @pl.when(e == 0)
    def _():
        acc_ref[...] = jnp.zeros_like(acc_ref)

    xb = x_ref[...].astype(BF16)
    hg = jnp.dot(xb, wg_ref[0], preferred_element_type=F32)
    hu = jnp.dot(xb, wu_ref[0], preferred_element_type=F32)
    h = hg * _sigmoid(hg) * hu
    y = jnp.dot(h.astype(BF16), wd_ref[0], preferred_element_type=F32)
    comb = comb_ref[...]
    ei = lax.broadcasted_iota(jnp.int32, comb.shape, 1)
    w = jnp.sum(jnp.where(ei == e, comb, 0.0), axis=-1, keepdims=True)
    acc_ref[...] += w * y

    @pl.when(e == pl.num_programs(1) - 1)
    def _():
        o_ref[...] = _layer_norm(ALPHA * x_ref[...] + acc_ref[...], g2_ref[...], b2_ref[...])


def _moe(x1, comb, lp, tm):
    t = x1.shape[0]
    row = lambda i, e: (i, 0)
    return pl.pallas_call(
        _moe_kernel,
        grid=(t // tm, N_EXPERTS),
        in_specs=[
            pl.BlockSpec((tm, D_MODEL), row),
            pl.BlockSpec((tm, N_EXPERTS), row),
            pl.BlockSpec((1, D_MODEL, D_EXPERT), lambda i, e: (e, 0, 0)),
            pl.BlockSpec((1, D_MODEL, D_EXPERT), lambda i, e: (e, 0, 0)),
            pl.BlockSpec((1, D_EXPERT, D_MODEL), lambda i, e: (e, 0, 0)),
            pl.BlockSpec((1, D_MODEL), lambda i, e: (0, 0)),
            pl.BlockSpec((1, D_MODEL), lambda i, e: (0, 0)),
        ],
        out_specs=pl.BlockSpec((tm, D_MODEL), row),
        out_shape=jax.ShapeDtypeStruct((t, D_MODEL), F32),
        scratch_shapes=[pltpu.VMEM((tm, D_MODEL), F32)],
        compiler_params=_params("parallel", "arbitrary"),
        name="moe",
    )(x1, comb, lp['w_gate'], lp['w_up'], lp['w_down'], lp['ln2_g'], lp['ln2_b'])


def _prompt_layer(x, lp, batch, seq_len, cos_t, sin_t):
    t = x.shape[0]
    nb = seq_len // MOBA_BLOCK
    k, v, u, qh, kh, vt, km = _inproj_prompt(x, lp['w_in'], cos_t, sin_t, seq_len)
    kh = kh.reshape(N_HEADS, t // MOBA_BLOCK, MOBA_BLOCK, HEAD_DIM)
    km_h = km.reshape(batch, nb, N_HEADS, HEAD_DIM).transpose(0, 2, 1, 3)
    attn = _moba_prompt(qh, kh, vt, km_h, batch, seq_len).T
    c = SSM_CHUNK
    u_g = (u.astype(BF16).reshape(t // c, c, SSM_GROUPS, SSM_GROUP_DIM)
           .transpose(2, 0, 1, 3).reshape(SSM_GROUPS, t // c, c * SSM_GROUP_DIM))
    y_g, hr, hi = _s5_prompt(u_g, lp['s5_prompt'], batch)
    y = (y_g.reshape(SSM_GROUPS, t // c, c, SSM_GROUP_DIM)
         .transpose(1, 2, 0, 3).reshape(t, SSM_WIDTH))
    x1, comb = _post_mixer(y, u, attn, x, lp, tm=512)
    x2 = _moe(x1, comb, lp, tm=1024)
    return x2, k, v, hr.transpose(1, 0, 2), hi.transpose(1, 0, 2)


def _sample_layer(x, lp, bsz, nq, cos_t, sin_t, cache_k, cache_v, page_table, layer, n_pool,
                  h0r, h0i):
    t = x.shape[0]
    q, k, v, u = _inproj_sample(x, lp['w_in'], cos_t, sin_t)
    to3 = lambda a: a.reshape(bsz, nq, ATTN_WIDTH)
    attn = _moba_sample(to3(q), to3(k), to3(v), cache_k, cache_v, page_table, layer, n_pool)
    attn = attn.reshape(t, ATTN_WIDTH)
    u_t = u.reshape(bsz, nq, SSM_WIDTH).transpose(1, 0, 2)
    y_t, hr, hi = _s5_sample(u_t, h0r, h0i, lp['s5_sample'])
    y = y_t.transpose(1, 0, 2).reshape(t, SSM_WIDTH)
    x1, comb = _post_mixer(y, u, attn, x, lp, tm=256)
    x2 = _moe(x1, comb, lp, tm=t)
    return x2, k, v, hr, hi


def _layer_params(l, w_in, w_out, norm_attn_g, norm_ssm_g, ssm_a_re, ssm_a_im, ssm_log_dt,
                  ssm_b_re, ssm_b_im, ssm_c_re, ssm_c_im, ssm_d, w_glu, b_glu, ln1_g, ln1_b,
                  w_group, b_group, w_expert, b_expert, w_gate, w_up, w_down, ln2_g, ln2_b):
    rowv = lambda a: a[l].reshape(1, -1).astype(F32)
    s5_args = (ssm_a_re[l], ssm_a_im[l], ssm_log_dt[l], ssm_b_re[l], ssm_b_im[l], ssm_c_re[l], ssm_c_im[l])
    pad = LANES - N_GROUPS - N_EXPERTS
    w_router = jnp.concatenate([w_group[l], w_expert[l], jnp.zeros((D_MODEL, pad), F32)], axis=1)
    b_router = jnp.concatenate([b_group[l], b_expert[l], jnp.zeros((pad,), F32)]).reshape(1, LANES)
    return dict(
        w_in=w_in[l].astype(BF16), w_out=w_out[l].astype(BF16), w_glu=w_glu[l].astype(BF16),
        g_attn=rowv(norm_attn_g), g_ssm=rowv(norm_ssm_g), d=rowv(ssm_d), b_glu=rowv(b_glu),
        ln1_g=rowv(ln1_g), ln1_b=rowv(ln1_b), ln2_g=rowv(ln2_g), ln2_b=rowv(ln2_b),
        w_router=w_router.astype(F32), b_router=b_router.astype(F32),
        w_gate=w_gate[l].astype(BF16), w_up=w_up[l].astype(BF16), w_down=w_down[l].astype(BF16),
        s5_prompt=_s5_prompt_mats(*s5_args), s5_sample=_s5_sample_mats(*s5_args),
    )


def kernel(x_prompt, x_sample, cache_k, cache_v, state_ssm_re, state_ssm_im, page_table, w_in, w_out, norm_attn_g, norm_ssm_g, ssm_a_re, ssm_a_im, ssm_log_dt, ssm_b_re, ssm_b_im, ssm_c_re, ssm_c_im, ssm_d, w_glu, b_glu, ln1_g, ln1_b, w_group, b_group, w_expert, b_expert, w_gate, w_up, w_down, ln2_g, ln2_b):
    batch, seq_len, _ = x_prompt.shape
    dec_batch, dec_seq, _ = x_sample.shape
    depth, n_pool = cache_k.shape[0], cache_k.shape[1]
    n_pages = page_table.shape[1]
    past_len = n_pages * PAGE_SIZE
    assert seq_len % 512 == 0 and past_len % MOBA_BLOCK == 0 and dec_seq <= MOBA_BLOCK

    cos_p, sin_p = _rope_tables(jnp.arange(seq_len, dtype=jnp.int32))
    pos_s = past_len + jnp.arange(dec_seq, dtype=jnp.int32)
    cos_s, sin_s = _rope_tables(jnp.tile(pos_s, dec_batch))

    ck = cache_k.reshape(depth * n_pool, PAGE_SIZE, ATTN_WIDTH)
    cv = cache_v.reshape(depth * n_pool, PAGE_SIZE, ATTN_WIDTH)
    xp = x_prompt.reshape(batch * seq_len, D_MODEL)
    xs = x_sample.reshape(dec_batch * dec_seq, D_MODEL)
    outs = [[] for _ in range(8)]
    for l in range(depth):
        lp = _layer_params(l, w_in, w_out, norm_attn_g, norm_ssm_g, ssm_a_re, ssm_a_im, ssm_log_dt,
                           ssm_b_re, ssm_b_im, ssm_c_re, ssm_c_im, ssm_d, w_glu, b_glu, ln1_g, ln1_b,
                           w_group, b_group, w_expert, b_expert, w_gate, w_up, w_down, ln2_g, ln2_b)
        xp, kp, vp, hrp, hip = _prompt_layer(xp, lp, batch, seq_len, cos_p, sin_p)
        h0r = state_ssm_re[l].reshape(dec_batch, SSM_GROUPS * SSM_STATE)
        h0i = state_ssm_im[l].reshape(dec_batch, SSM_GROUPS * SSM_STATE)
        xs, kn, vn, hrs, his = _sample_layer(xs, lp, dec_batch, dec_seq, cos_s, sin_s, ck, cv,
                                             page_table, l, n_pool, h0r, h0i)
        outs[0].append(kp.reshape(batch, seq_len, N_HEADS, HEAD_DIM))
        outs[1].append(vp.reshape(batch, seq_len, N_HEADS, HEAD_DIM))
        outs[2].append(hrp)
        outs[3].append(hip)
        outs[4].append(kn.reshape(dec_batch, dec_seq, N_HEADS, HEAD_DIM))
        outs[5].append(vn.reshape(dec_batch, dec_seq, N_HEADS, HEAD_DIM))
        outs[6].append(hrs.reshape(dec_batch, SSM_GROUPS, SSM_STATE))
        outs[7].append(his.reshape(dec_batch, SSM_GROUPS, SSM_STATE))
    return (xp.reshape(batch, seq_len, D_MODEL), xs.reshape(dec_batch, dec_seq, D_MODEL),
            *[jnp.stack(o) for o in outs])
```

```python
import functools
import math

import jax
import jax.numpy as jnp
from jax import lax
from jax.experimental import pallas as pl
from jax.experimental.pallas import tpu as pltpu

F32 = jnp.float32
BF16 = jnp.bfloat16
HIGHEST = lax.Precision.HIGHEST

D_MODEL = 1024
ATTN_WIDTH = 512
SSM_WIDTH = 512
HEAD_DIM = 64
N_HEADS = 8
MOBA_BLOCK = 256
MOBA_TOPK = 3
PAGE_SIZE = 128
ROPE_THETA = 10000.0
SSM_GROUP_DIM = 16
SSM_GROUPS = 32
SSM_STATE = 64
N_GROUPS = 4
EXPERTS_PER_GROUP = 4
N_EXPERTS = 16
D_EXPERT = 512
DEPTH = 4
ALPHA = (2.0 * DEPTH) ** 0.25
LN_EPS = 1e-5
RMS_EPS = 1e-6
NEG_INF = -1e30
PROJ_WIDTH = 3 * ATTN_WIDTH + SSM_WIDTH

LANES = 128
SSM_CHUNK = 32
VMEM_LIMIT = 48 * 1024 * 1024
SAMPLE_ATTN_VMEM_LIMIT = 56 * 1024 * 1024


def _params(*sem):
    return pltpu.CompilerParams(dimension_semantics=sem, vmem_limit_bytes=VMEM_LIMIT)


def _rope_tables(pos):
    inv_freq = 1.0 / jnp.power(ROPE_THETA, jnp.arange(0, HEAD_DIM, 2, dtype=F32) / HEAD_DIM)
    ang = pos.astype(F32)[:, None] * inv_freq[None, :]
    cos, sin = jnp.cos(ang), jnp.sin(ang)
    cos_t = jnp.concatenate([cos, cos, cos, cos], axis=-1)
    sin_t = jnp.concatenate([-sin, sin, -sin, sin], axis=-1)
    return cos_t, sin_t


def _rope(x, cos, sin):
    tm = x.shape[0]
    lane = lax.broadcasted_iota(jnp.int32, (tm, LANES), 1)
    first_half = (lane % HEAD_DIM) < (HEAD_DIM // 2)
    outs = []
    for c in range(x.shape[1] // LANES):
        xc = x[:, c * LANES:(c + 1) * LANES]
        partner = jnp.where(first_half, pltpu.roll(xc, LANES - HEAD_DIM // 2, 1),
                            pltpu.roll(xc, HEAD_DIM // 2, 1))
        outs.append(xc * cos + partner * sin)
    return jnp.concatenate(outs, axis=1)


def _inproj_prompt_kernel(x_ref, w_ref, cos_ref, sin_ref, *rest):
    k_ref, v_ref, u_ref, q_ref, kb_ref, vt_ref, km_ref = rest[-7:]
    tm = x_ref.shape[0]
    proj = jnp.dot(x_ref[...].astype(BF16), w_ref[...], preferred_element_type=F32)
    cos, sin = cos_ref[...], sin_ref[...]
    q = _rope(proj[:, :ATTN_WIDTH], cos, sin)
    k = _rope(proj[:, ATTN_WIDTH:2 * ATTN_WIDTH], cos, sin)
    v = proj[:, 2 * ATTN_WIDTH:3 * ATTN_WIDTH]
    for h in range(N_HEADS):
        k_ref[0, :, h, :] = k[:, h * HEAD_DIM:(h + 1) * HEAD_DIM]
        v_ref[0, :, h, :] = v[:, h * HEAD_DIM:(h + 1) * HEAD_DIM]
    u_ref[...] = proj[:, 3 * ATTN_WIDTH:]
    q_ref[...] = q * (HEAD_DIM ** -0.5)
    kb_ref[...] = k.astype(BF16)
    vt = v.T
    for blk in range(tm // MOBA_BLOCK):
        sl = slice(blk * MOBA_BLOCK, (blk + 1) * MOBA_BLOCK)
        vt_ref[:, blk] = vt[:, sl].reshape(N_HEADS, HEAD_DIM, MOBA_BLOCK).astype(BF16)
        km_ref[blk] = jnp.mean(k[sl], axis=0, keepdims=True)


def _inproj_prompt(x, w_in, cos_t, sin_t, seq_len, layer, depth, kbuf, vbuf, tm=512):
    t = x.shape[0]
    n_tiles_seq = seq_len // tm
    nblk = t // MOBA_BLOCK
    bpt = tm // MOBA_BLOCK
    row = lambda i: (i, 0)
    kv_sds = jax.ShapeDtypeStruct((depth, t, N_HEADS, HEAD_DIM), F32)
    out_shape = (
        kv_sds,
        kv_sds,
        jax.ShapeDtypeStruct((t, SSM_WIDTH), F32),
        jax.ShapeDtypeStruct((t, ATTN_WIDTH), F32),
        jax.ShapeDtypeStruct((t, ATTN_WIDTH), BF16),
        jax.ShapeDtypeStruct((N_HEADS, nblk, HEAD_DIM, MOBA_BLOCK), BF16),
        jax.ShapeDtypeStruct((nblk, 1, ATTN_WIDTH), F32),
    )
    kv_spec = pl.BlockSpec((1, tm, N_HEADS, HEAD_DIM), lambda i: (layer, i, 0, 0))
    any_spec = pl.BlockSpec(memory_space=pl.ANY)
    in_specs = [
        pl.BlockSpec((tm, D_MODEL), row),
        pl.BlockSpec((D_MODEL, PROJ_WIDTH), lambda i: (0, 0)),
        pl.BlockSpec((tm, LANES), lambda i: (i % n_tiles_seq, 0)),
        pl.BlockSpec((tm, LANES), lambda i: (i % n_tiles_seq, 0)),
    ]
    out_specs = (
        kv_spec, kv_spec,
        pl.BlockSpec((tm, SSM_WIDTH), row),
        pl.BlockSpec((tm, ATTN_WIDTH), row),
        pl.BlockSpec((tm, ATTN_WIDTH), row),
        pl.BlockSpec((N_HEADS, bpt, HEAD_DIM, MOBA_BLOCK), lambda i: (0, i, 0, 0)),
        pl.BlockSpec((bpt, 1, ATTN_WIDTH), lambda i: (i, 0, 0)),
    )
    return pl.pallas_call(
        _inproj_prompt_kernel,
        grid=(t // tm,),
        in_specs=in_specs + [any_spec, any_spec],
        out_specs=out_specs,
        out_shape=out_shape,
        input_output_aliases={4: 0, 5: 1},
        compiler_params=_params("parallel"),
        name="inproj_prompt",
    )(x, w_in, cos_t, sin_t, kbuf, vbuf)


def _inproj_sample_kernel(x_ref, w_ref, cos_ref, sin_ref, q_ref, k_ref, v_ref, u_ref):
    proj = jnp.dot(x_ref[...].astype(BF16), w_ref[...], preferred_element_type=F32)
    cos, sin = cos_ref[...], sin_ref[...]
    q_ref[...] = _rope(proj[:, :ATTN_WIDTH], cos, sin) * (HEAD_DIM ** -0.5)
    k_ref[...] = _rope(proj[:, ATTN_WIDTH:2 * ATTN_WIDTH], cos, sin)
    v_ref[...] = proj[:, 2 * ATTN_WIDTH:3 * ATTN_WIDTH]
    u_ref[...] = proj[:, 3 * ATTN_WIDTH:]


def _inproj_sample(x, w_in, cos_t, sin_t, tm=256):
    t = x.shape[0]
    tm = min(tm, t)
    row = lambda i: (i, 0)
    sds = jax.ShapeDtypeStruct((t, ATTN_WIDTH), F32)
    return pl.pallas_call(
        _inproj_sample_kernel,
        grid=(t // tm,),
        in_specs=[
            pl.BlockSpec((tm, D_MODEL), row),
            pl.BlockSpec((D_MODEL, PROJ_WIDTH), lambda i: (0, 0)),
            pl.BlockSpec((tm, LANES), row),
            pl.BlockSpec((tm, LANES), row),
        ],
        out_specs=tuple(pl.BlockSpec((tm, ATTN_WIDTH), row) for _ in range(4)),
        out_shape=(sds, sds, sds, sds),
        compiler_params=_params("parallel"),
        name="inproj_sample",
    )(x, w_in, cos_t, sin_t)


def _select_blocks(gate, n_valid, axis):
    nb = gate.shape[axis]
    blk = lax.broadcasted_iota(jnp.int32, gate.shape, axis)
    gate = jnp.where(blk < n_valid, gate, NEG_INF)
    sel = jnp.zeros(gate.shape, dtype=jnp.bool_)
    for r in range(MOBA_TOPK):
        top = jnp.max(gate, axis=axis, keepdims=True)
        first = jnp.min(jnp.where(gate == top, blk, nb), axis=axis, keepdims=True)
        hit = blk == first
        sel = jnp.logical_or(sel, jnp.logical_and(hit, r < n_valid))
        gate = jnp.where(hit, -jnp.inf, gate)
    return jnp.where(sel, 0.0, NEG_INF).astype(F32)


HEADS_PER_STEP = 4


def _moba_prompt_kernel(q_ref, k_ref, vt_ref, km_ref, o_ref, *scratch):
    hps = HEADS_PER_STEP
    qm_refs, bias_refs = scratch[:hps], scratch[hps:2 * hps]
    m_refs, l_refs, acc_refs = scratch[2 * hps:3 * hps], scratch[3 * hps:4 * hps], scratch[4 * hps:5 * hps]
    s_bufs = (scratch[5 * hps:6 * hps], scratch[6 * hps:7 * hps])
    qi = pl.program_id(2)
    q = q_ref[...]
    lane_head = lax.broadcasted_iota(jnp.int32, q.shape, 1) // HEAD_DIM
    km = km_ref[0]
    nt = (((1,), (1,)), ((), ()))
    for h in range(hps):
        qm = jnp.where(lane_head == h, q, 0.0)
        gate = lax.dot_general(km, qm, nt, precision=HIGHEST, preferred_element_type=F32)
        bias_refs[h][...] = _select_blocks(gate, qi, axis=0)
        qm_refs[h][...] = qm.astype(BF16)

    k_own = k_ref[pl.ds(pl.multiple_of(qi * MOBA_BLOCK, MOBA_BLOCK), MOBA_BLOCK), :]
    for h in range(hps):
        s = lax.dot_general(k_own, qm_refs[h][...], nt, preferred_element_type=F32)
        key_i = lax.broadcasted_iota(jnp.int32, s.shape, 0)
        qry_i = lax.broadcasted_iota(jnp.int32, s.shape, 1)
        s = jnp.where(key_i <= qry_i, s, NEG_INF)
        m0 = jnp.max(s, axis=0, keepdims=True)
        p = jnp.exp(s - m0)
        m_refs[h][...] = m0
        l_refs[h][...] = jnp.sum(p, axis=0, keepdims=True)
        acc_refs[h][...] = jnp.dot(vt_ref[h, qi], p.astype(BF16), preferred_element_type=F32)

    last = jnp.maximum(qi - 1, 0)

    def issue_scores(j, bufs):
        j = jnp.minimum(j, last)
        k_j = k_ref[pl.ds(pl.multiple_of(j * MOBA_BLOCK, MOBA_BLOCK), MOBA_BLOCK), :]
        for h in range(hps):
            bufs[h][...] = lax.dot_general(k_j, qm_refs[h][...], nt, preferred_element_type=F32)

    def consume(j, bufs):
        for h in range(hps):
            s = bufs[h][...] + bias_refs[h][pl.ds(j, 1), :]
            m = m_refs[h][...]
            m_new = jnp.maximum(m, jnp.max(s, axis=0, keepdims=True))
            a = jnp.exp(m - m_new)
            p = jnp.exp(s - m_new)
            m_refs[h][...] = m_new
            l_refs[h][...] = a * l_refs[h][...] + jnp.sum(p, axis=0, keepdims=True)
            acc_refs[h][...] = a * acc_refs[h][...] + jnp.dot(vt_ref[h, j], p.astype(BF16),
                                                             preferred_element_type=F32)

    issue_scores(0, s_bufs[0])

    def pair(i, carry):
        j = 2 * i
        issue_scores(j + 1, s_bufs[1])
        consume(j, s_bufs[0])
        issue_scores(j + 2, s_bufs[0])
        consume(j + 1, s_bufs[1])
        return carry

    lax.fori_loop(0, qi // 2, pair, 0)

    @pl.when(qi % 2 == 1)
    def _():
        consume(qi - 1, s_bufs[0])

    out_t = jnp.concatenate([acc_refs[h][...] / l_refs[h][...] for h in range(hps)], axis=0)
    o_ref[...] = out_t.T


def _moba_prompt(q, kb, vt, km, batch, seq_len):
    t = q.shape[0]
    nb = seq_len // MOBA_BLOCK
    tq = MOBA_BLOCK
    hps = HEADS_PER_STEP
    gw = hps * HEAD_DIM
    scratch = ([pltpu.VMEM((tq, gw), BF16) for _ in range(hps)]
               + [pltpu.VMEM((nb, tq), F32) for _ in range(hps)]
               + [pltpu.VMEM((1, tq), F32) for _ in range(2 * hps)]
               + [pltpu.VMEM((HEAD_DIM, tq), F32) for _ in range(hps)]
               + [pltpu.VMEM((MOBA_BLOCK, tq), F32) for _ in range(2 * hps)])
    return pl.pallas_call(
        _moba_prompt_kernel,
        grid=(batch, N_HEADS // hps, nb),
        in_specs=[
            pl.BlockSpec((tq, gw), lambda b, g, i: (b * nb + i, g)),
            pl.BlockSpec((seq_len, gw), lambda b, g, i: (b, g)),
            pl.BlockSpec((hps, nb, HEAD_DIM, MOBA_BLOCK), lambda b, g, i: (g, b, 0, 0)),
            pl.BlockSpec((1, nb, gw), lambda b, g, i: (b, 0, g)),
        ],
        out_specs=pl.BlockSpec((tq, gw), lambda b, g, i: (b * nb + i, g)),
        out_shape=jax.ShapeDtypeStruct((t, ATTN_WIDTH), F32),
        scratch_shapes=scratch,
        compiler_params=_params("parallel", "parallel", "arbitrary"),
        name="moba_prompt",
    )(q, kb, vt, km)


def _moba_sample_kernel(n_pages, pt_ref, q_ref, kn_ref, vn_ref, *rest):
    del pt_ref
    k_pages = rest[:n_pages]
    v_pages = rest[n_pages:2 * n_pages]
    o_ref, s_ref = rest[2 * n_pages], rest[2 * n_pages + 1]
    rows = q_ref.shape[1]
    nq = rows // N_HEADS
    kw = PAGE_SIZE * N_HEADS
    ppb = MOBA_BLOCK // PAGE_SIZE
    nb_past = n_pages // ppb
    nt = (((1,), (1,)), ((), ()))

    q = q_ref[0]
    qb = q.astype(BF16)

    blk_lane = lax.broadcasted_iota(jnp.int32, (rows, nb_past), 1)
    gate = jnp.zeros((rows, nb_past), F32)
    for n in range(nb_past):
        tot = jnp.zeros((N_HEADS, HEAD_DIM), F32)
        for pg in range(n * ppb, (n + 1) * ppb):
            tot = tot + jnp.sum(k_pages[pg][0], axis=0)
        kmean = jnp.concatenate([tot * (1.0 / MOBA_BLOCK)] * nq, axis=0)
        gate = jnp.where(blk_lane == n, jnp.sum(q * kmean, axis=1, keepdims=True), gate)
    bias = _select_blocks(gate, nb_past, axis=1)

    row_head = lax.broadcasted_iota(jnp.int32, (rows, kw), 0) % N_HEADS
    col_head = lax.broadcasted_iota(jnp.int32, (rows, kw), 1) % N_HEADS
    same_head = row_head == col_head
    m_el = jnp.full((rows, kw), NEG_INF, F32)
    for pg in range(n_pages):
        k2 = k_pages[pg][0].reshape(kw, HEAD_DIM).astype(BF16)
        s = lax.dot_general(qb, k2, nt, preferred_element_type=F32)
        n = pg // ppb
        s = jnp.where(same_head, s + bias[:, n:n + 1], NEG_INF)
        s_ref[pg] = s
        m_el = jnp.maximum(m_el, s)
    sn = lax.dot_general(qb, kn_ref[0].astype(BF16), nt, preferred_element_type=F32)
    r_i = lax.broadcasted_iota(jnp.int32, (rows, rows), 0)
    c_i = lax.broadcasted_iota(jnp.int32, (rows, rows), 1)
    ok = jnp.logical_and(r_i % N_HEADS == c_i % N_HEADS, c_i // N_HEADS <= r_i // N_HEADS)
    sn = jnp.where(ok, sn, NEG_INF)

    m = jnp.maximum(jnp.max(m_el, axis=1, keepdims=True), jnp.max(sn, axis=1, keepdims=True))
    pn = jnp.exp(sn - m)
    l_new = jnp.sum(pn, axis=1, keepdims=True)
    acc = jnp.dot(pn.astype(BF16), vn_ref[0].astype(BF16), preferred_element_type=F32)
    l_el = jnp.zeros((rows, kw), F32)
    for pg in range(n_pages):
        p = jnp.exp(s_ref[pg] - m)
        l_el = l_el + p
        v2 = v_pages[pg][0].reshape(kw, HEAD_DIM).astype(BF16)
        acc = acc + jnp.dot(p.astype(BF16), v2, preferred_element_type=F32)
    o_ref[0] = acc / (l_new + jnp.sum(l_el, axis=1, keepdims=True))


def _moba_sample(q, k_new, v_new, cache_k, cache_v, page_table, layer, n_pool):
    bsz, rows, _ = q.shape
    n_pages = page_table.shape[1]
    base = layer * n_pool
    tok = pl.BlockSpec((1, rows, HEAD_DIM), lambda b, pt: (b, 0, 0))

    def page_spec(pg):
        return pl.BlockSpec((1, PAGE_SIZE, N_HEADS, HEAD_DIM), lambda b, pt: (base + pt[b, pg], 0, 0, 0))

    grid_spec = pltpu.PrefetchScalarGridSpec(
        num_scalar_prefetch=1,
        grid=(bsz,),
        in_specs=[tok, tok, tok] + [page_spec(pg) for pg in range(n_pages)] * 2,
        out_specs=tok,
        scratch_shapes=[pltpu.VMEM((n_pages, rows, PAGE_SIZE * N_HEADS), F32)],
    )
    return pl.pallas_call(
        functools.partial(_moba_sample_kernel, n_pages),
        grid_spec=grid_spec,
        out_shape=jax.ShapeDtypeStruct((bsz, rows, HEAD_DIM), F32),
        compiler_params=pltpu.CompilerParams(dimension_semantics=("parallel",),
                                             vmem_limit_bytes=SAMPLE_ATTN_VMEM_LIMIT),
        name="moba_sample",
    )(page_table, q, k_new, v_new, *([cache_k] * n_pages), *([cache_v] * n_pages))


def _s5_discretise(a_re, a_im, log_dt, b_re, b_im):
    dt = jnp.exp(log_dt.astype(F32))[:, None]
    ar, ai = a_re.astype(F32), a_im.astype(F32)
    mag = jnp.exp(ar * dt)
    lr, li = mag * jnp.cos(ai * dt), mag * jnp.sin(ai * dt)
    den = ar * ar + ai * ai
    nr = lr - 1.0
    fr = (nr * ar + li * ai) / den
    fi = (li * ar - nr * ai) / den
    br_, bi_ = b_re.astype(F32), b_im.astype(F32)
    bbr = fr[..., None] * br_ - fi[..., None] * bi_
    bbi = fr[..., None] * bi_ + fi[..., None] * br_
    return ar * dt, ai * dt, lr, li, bbr, bbi


def _s5_prompt_mats(a_re, a_im, log_dt, b_re, b_im, c_re, c_im):
    c = SSM_CHUNK
    g_, p_, cdim = SSM_GROUPS, SSM_STATE, SSM_GROUP_DIM
    adr, adi, _, _, bbr, bbi = _s5_discretise(a_re, a_im, log_dt, b_re, b_im)
    tau = jnp.arange(c + 1, dtype=F32)[:, None, None]
    pmag = jnp.exp(adr[None] * tau)
    pw_r, pw_i = pmag * jnp.cos(adi[None] * tau), pmag * jnp.sin(adi[None] * tau)
    cr, ci = c_re.astype(F32), c_im.astype(F32)
    lb_r = pw_r[..., None] * bbr[None] - pw_i[..., None] * bbi[None]
    lb_i = pw_r[..., None] * bbi[None] + pw_i[..., None] * bbr[None]
    kern = (jnp.einsum('gop,tgpi->tgoi', cr, lb_r[:c], precision=HIGHEST)
            - jnp.einsum('gop,tgpi->tgoi', ci, lb_i[:c], precision=HIGHEST))
    s_i = jnp.arange(c)[:, None]
    t_i = jnp.arange(c)[None, :]
    lag = t_i - s_i
    toep = jnp.where((lag >= 0)[:, :, None, None, None], kern[jnp.clip(lag, 0, c - 1)], 0.0)
    toep = toep.transpose(2, 0, 4, 1, 3).reshape(g_, c * cdim, c * cdim)
    w_r = lb_r[:c][::-1].transpose(1, 0, 3, 2).reshape(g_, c * cdim, p_)
    w_i = lb_i[:c][::-1].transpose(1, 0, 3, 2).reshape(g_, c * cdim, p_)
    pr, pi = pw_r[1:], pw_i[1:]
    v_r = cr[None] * pr[:, :, None, :] - ci[None] * pi[:, :, None, :]
    v_i = -cr[None] * pi[:, :, None, :] - ci[None] * pr[:, :, None, :]
    v_r = v_r.transpose(1, 3, 0, 2).reshape(g_, p_, c * cdim)
    v_i = v_i.transpose(1, 3, 0, 2).reshape(g_, p_, c * cdim)
    decay = jnp.stack([pw_r[c], pw_i[c]], axis=1)
    return toep.astype(BF16), w_r.astype(BF16), w_i.astype(BF16), v_r.astype(BF16), v_i.astype(BF16), decay


def _s5_sample_mats(a_re, a_im, log_dt, b_re, b_im, c_re, c_im):
    g_, p_, cdim = SSM_GROUPS, SSM_STATE, SSM_GROUP_DIM
    _, _, lr, li, bbr, bbi = _s5_discretise(a_re, a_im, log_dt, b_re, b_im)
    eye = jnp.eye(g_, dtype=F32)
    b_bd_r = (eye[:, None, :, None] * bbr.transpose(0, 2, 1)[:, :, None, :]).reshape(g_ * cdim, g_ * p_)
    b_bd_i = (eye[:, None, :, None] * bbi.transpose(0, 2, 1)[:, :, None, :]).reshape(g_ * cdim, g_ * p_)
    cr, ci = c_re.astype(F32), c_im.astype(F32)
    c_bd_r = (eye[:, None, :, None] * cr.transpose(0, 2, 1)[:, :, None, :]).reshape(g_ * p_, g_ * cdim)
    c_bd_i = (eye[:, None, :, None] * ci.transpose(0, 2, 1)[:, :, None, :]).reshape(g_ * p_, g_ * cdim)
    return lr.reshape(1, g_ * p_), li.reshape(1, g_ * p_), b_bd_r, b_bd_i, c_bd_r, c_bd_i


def _s5_prompt_kernel(n_batch, u_ref, toep_ref, wr_ref, wi_ref, vr_ref, vi_ref, dec_ref,
                      y_ref, hr_ref, hi_ref, sr_s, si_s, pr_s, pi_s):
    u = u_ref[0]
    n_chunks = u.shape[0]
    cpb = n_chunks // n_batch
    sr_s[...] = jnp.dot(u, wr_ref[0], preferred_element_type=F32)
    si_s[...] = jnp.dot(u, wi_ref[0], preferred_element_type=F32)
    dr = dec_ref[0, 0:1, :]
    di = dec_ref[0, 1:2, :]

    def step(kk, carry):
        new = []
        for b in range(n_batch):
            hr, hi = carry[2 * b], carry[2 * b + 1]
            r = b * cpb + kk
            pr_s[pl.ds(r, 1), :] = hr
            pi_s[pl.ds(r, 1), :] = hi
            new.append(dr * hr - di * hi + sr_s[pl.ds(r, 1), :])
            new.append(dr * hi + di * hr + si_s[pl.ds(r, 1), :])
        return tuple(new)

    zero = jnp.zeros((1, SSM_STATE), F32)
    fin = lax.fori_loop(0, cpb, step, (zero,) * (2 * n_batch))
    for b in range(n_batch):
        hr_ref[0, b:b + 1, :] = fin[2 * b]
        hi_ref[0, b:b + 1, :] = fin[2 * b + 1]
    y = jnp.dot(u, toep_ref[0], preferred_element_type=F32)
    y = y + jnp.dot(pr_s[...].astype(BF16), vr_ref[0], preferred_element_type=F32)
    y = y + jnp.dot(pi_s[...].astype(BF16), vi_ref[0], preferred_element_type=F32)
    y_ref[0] = y


def _s5_prompt(u_g, mats, n_batch):
    toep, w_r, w_i, v_r, v_i, decay = mats
    g_, n_chunks, cw = u_g.shape
    per_g = lambda *shape: pl.BlockSpec((1,) + shape, lambda g: (g,) + (0,) * len(shape))
    st = jax.ShapeDtypeStruct((g_, n_batch, SSM_STATE), F32)
    return pl.pallas_call(
        functools.partial(_s5_prompt_kernel, n_batch),
        grid=(g_,),
        in_specs=[per_g(n_chunks, cw), per_g(cw, cw), per_g(cw, SSM_STATE), per_g(cw, SSM_STATE),
                  per_g(SSM_STATE, cw), per_g(SSM_STATE, cw), per_g(2, SSM_STATE)],
        out_specs=(per_g(n_chunks, cw), per_g(n_batch, SSM_STATE), per_g(n_batch, SSM_STATE)),
        out_shape=(jax.ShapeDtypeStruct((g_, n_chunks, cw), F32), st, st),
        scratch_shapes=[pltpu.VMEM((n_chunks, SSM_STATE), F32) for _ in range(4)],
        compiler_params=_params("parallel"),
        name="s5_prompt",
    )(u_g, toep, w_r, w_i, v_r, v_i, decay)


def _s5_sample_kernel(u_ref, h0r_ref, h0i_ref, lr_ref, li_ref, br_ref, bi_ref, cr_ref, ci_ref,
                      y_ref, hr_ref, hi_ref):
    n_steps = u_ref.shape[0]
    lr, li = lr_ref[...], li_ref[...]
    hr, hi = h0r_ref[...], h0i_ref[...]
    for t in range(n_steps):
        u = u_ref[t]
        bur = jnp.dot(u, br_ref[...], precision=HIGHEST, preferred_element_type=F32)
        bui = jnp.dot(u, bi_ref[...], precision=HIGHEST, preferred_element_type=F32)
        hr, hi = lr * hr - li * hi + bur, lr * hi + li * hr + bui
        y_ref[t] = (jnp.dot(hr, cr_ref[...], precision=HIGHEST, preferred_element_type=F32)
                    - jnp.dot(hi, ci_ref[...], precision=HIGHEST, preferred_element_type=F32))
    hr_ref[...] = hr
    hi_ref[...] = hi


def _s5_sample(u_t, h0r, h0i, mats):
    n_steps, bsz, _ = u_t.shape
    st = jax.ShapeDtypeStruct(h0r.shape, F32)
    return pl.pallas_call(
        _s5_sample_kernel,
        out_shape=(jax.ShapeDtypeStruct((n_steps, bsz, SSM_WIDTH), F32), st, st),
        compiler_params=pltpu.CompilerParams(vmem_limit_bytes=VMEM_LIMIT),
        name="s5_sample",
    )(u_t, h0r, h0i, *mats)


def _layer_norm(x, g, b):
    mu = jnp.mean(x, axis=-1, keepdims=True)
    xc = x - mu
    var = jnp.mean(xc * xc, axis=-1, keepdims=True)
    return xc * lax.rsqrt(var + LN_EPS) * g + b


def _rms_gain(y, g):
    return y * lax.rsqrt(jnp.mean(y * y, axis=-1, keepdims=True) + RMS_EPS) * g


def _sigmoid(x):
    return 1.0 / (1.0 + jnp.exp(-x))


def _gelu_tanh(x):
    return 0.5 * x * (1.0 + jnp.tanh(math.sqrt(2.0 / math.pi) * (x + 0.044715 * (x * x * x))))


def _route(lg, le):
    tm = lg.shape[0]
    gi = lax.broadcasted_iota(jnp.int32, lg.shape, 1)
    gmax = jnp.max(lg, axis=-1, keepdims=True)
    g_sel = jnp.min(jnp.where(lg == gmax, gi, N_GROUPS), axis=-1, keepdims=True)
    p_sel = 1.0 / jnp.sum(jnp.exp(lg - gmax), axis=-1, keepdims=True)
    ei = lax.broadcasted_iota(jnp.int32, (tm, N_EXPERTS), 1)
    cand = jnp.where((ei // EXPERTS_PER_GROUP) == g_sel, le, -jnp.inf)
    v0 = jnp.max(cand, axis=-1, keepdims=True)
    i0 = jnp.min(jnp.where(cand == v0, ei, N_EXPERTS), axis=-1, keepdims=True)
    cand = jnp.where(ei == i0, -jnp.inf, cand)
    v1 = jnp.max(cand, axis=-1, keepdims=True)
    i1 = jnp.min(jnp.where(cand == v1, ei, N_EXPERTS), axis=-1, keepdims=True)
    e1 = jnp.exp(v1 - v0)
    w0 = p_sel / (1.0 + e1)
    w1 = p_sel * e1 / (1.0 + e1)
    return jnp.where(ei == i0, w0, 0.0) + jnp.where(ei == i1, w1, 0.0)


def _post_mixer_kernel(y_ref, u_ref, a_ref, x_ref, d_ref, wglu_ref, bglu_ref, ga_ref, gs_ref,
                       wout_ref, g1_ref, b1_ref, wr_ref, br_ref, x1_ref, comb_ref):
    y = y_ref[...] + d_ref[...] * u_ref[...]
    g = _gelu_tanh(y)
    z = g * _sigmoid(jnp.dot(g.astype(BF16), wglu_ref[...], preferred_element_type=F32) + bglu_ref[...])
    mix_in = jnp.concatenate([_rms_gain(a_ref[...], ga_ref[...]), _rms_gain(z, gs_ref[...])], axis=-1)
    mixed = jnp.dot(mix_in.astype(BF16), wout_ref[...], preferred_element_type=F32)
    x1 = _layer_norm(ALPHA * x_ref[...] + mixed, g1_ref[...], b1_ref[...])
    x1_ref[...] = x1
    logits = jnp.dot(x1, wr_ref[...], precision=HIGHEST, preferred_element_type=F32) + br_ref[...]
    comb_ref[...] = _route(logits[:, :N_GROUPS], logits[:, N_GROUPS:N_GROUPS + N_EXPERTS])


def _post_mixer(y, u, attn, x, lp, tm):
    t = x.shape[0]
    tm = min(tm, t)
    row = lambda i: (i, 0)
    full = lambda a: pl.BlockSpec(a.shape, lambda i: (0, 0))
    weights = [lp['d'], lp['w_glu'], lp['b_glu'], lp['g_attn'], lp['g_ssm'], lp['w_out'],
               lp['ln1_g'], lp['ln1_b'], lp['w_router'], lp['b_router']]
    return pl.pallas_call(
        _post_mixer_kernel,
        grid=(t // tm,),
        in_specs=[pl.BlockSpec((tm, SSM_WIDTH), row), pl.BlockSpec((tm, SSM_WIDTH), row),
                  pl.BlockSpec((tm, ATTN_WIDTH), row), pl.BlockSpec((tm, D_MODEL), row)]
                 + [full(w) for w in weights],
        out_specs=(pl.BlockSpec((tm, D_MODEL), row), pl.BlockSpec((tm, N_EXPERTS), row)),
        out_shape=(jax.ShapeDtypeStruct((t, D_MODEL), F32), jax.ShapeDtypeStruct((t, N_EXPERTS), F32)),
        compiler_params=_params("parallel"),
        name="post_mixer",
    )(y, u, attn, x, *weights)


def _moe_kernel(x_ref, comb_ref, wg_ref, wu_ref, wd_ref, g2_ref, b2_ref, o_ref, acc_ref):
    e = pl.program_id(1)

    @pl.when(e == 0)
    def _():
        acc_ref[...] = jnp.zeros_like(acc_ref)

    xb = x_ref[...].astype(BF16)
    hg = jnp.dot(xb, wg_ref[0], preferred_element_type=F32)
    hu = jnp.dot(xb, wu_ref[0], preferred_element_type=F32)
    h = hg * _sigmoid(hg) * hu
    y = jnp.dot(h.astype(BF16), wd_ref[0], preferred_element_type=F32)
    comb = comb_ref[...]
    ei = lax.broadcasted_iota(jnp.int32, comb.shape, 1)
    w = jnp.sum(jnp.where(ei == e, comb, 0.0), axis=-1, keepdims=True)
    acc_ref[...] += w * y

    @pl.when(e == pl.num_programs(1) - 1)
    def _():
        o_ref[...] = _layer_norm(ALPHA * x_ref[...] + acc_ref[...], g2_ref[...], b2_ref[...])


def _moe(x1, comb, lp, tm):
    t = x1.shape[0]
    row = lambda i, e: (i, 0)
    return pl.pallas_call(
        _moe_kernel,
        grid=(t // tm, N_EXPERTS),
        in_specs=[
            pl.BlockSpec((tm, D_MODEL), row),
            pl.BlockSpec((tm, N_EXPERTS), row),
            pl.BlockSpec((1, D_MODEL, D_EXPERT), lambda i, e: (e, 0, 0)),
            pl.BlockSpec((1, D_MODEL, D_EXPERT), lambda i, e: (e, 0, 0)),
            pl.BlockSpec((1, D_EXPERT, D_MODEL), lambda i, e: (e, 0, 0)),
            pl.BlockSpec((1, D_MODEL), lambda i, e: (0, 0)),
            pl.BlockSpec((1, D_MODEL), lambda i, e: (0, 0)),
        ],
        out_specs=pl.BlockSpec((tm, D_MODEL), row),
        out_shape=jax.ShapeDtypeStruct((t, D_MODEL), F32),
        scratch_shapes=[pltpu.VMEM((tm, D_MODEL), F32)],
        compiler_params=_params("parallel", "arbitrary"),
        name="moe",
    )(x1, comb, lp['w_gate'], lp['w_up'], lp['w_down'], lp['ln2_g'], lp['ln2_b'])


def _prompt_layer(x, lp, batch, seq_len, cos_t, sin_t, layer, depth, kbuf, vbuf):
    t = x.shape[0]
    nb = seq_len // MOBA_BLOCK
    kbuf, vbuf, u, q, kb, vt, km = _inproj_prompt(x, lp['w_in'], cos_t, sin_t, seq_len,
                                                   layer, depth, kbuf, vbuf)
    attn = _moba_prompt(q, kb, vt, km.reshape(batch, nb, ATTN_WIDTH), batch, seq_len)
    c = SSM_CHUNK
    u_g = (u.astype(BF16).reshape(t // c, c, SSM_GROUPS, SSM_GROUP_DIM)
           .transpose(2, 0, 1, 3).reshape(SSM_GROUPS, t // c, c * SSM_GROUP_DIM))
    y_g, hr, hi = _s5_prompt(u_g, lp['s5_prompt'], batch)
    y = (y_g.reshape(SSM_GROUPS, t // c, c, SSM_GROUP_DIM)
         .transpose(1, 2, 0, 3).reshape(t, SSM_WIDTH))
    x1, comb = _post_mixer(y, u, attn, x, lp, tm=512)
    x2 = _moe(x1, comb, lp, tm=1024)
    return x2, kbuf, vbuf, hr.transpose(1, 0, 2), hi.transpose(1, 0, 2)


def _sample_layer(x, lp, bsz, nq, cos_t, sin_t, cache_k, cache_v, page_table, layer, n_pool,
                  h0r, h0i):
    t = x.shape[0]
    q, k, v, u = _inproj_sample(x, lp['w_in'], cos_t, sin_t)
    to3 = lambda a: a.reshape(bsz, nq * N_HEADS, HEAD_DIM)
    attn = _moba_sample(to3(q), to3(k), to3(v), cache_k, cache_v, page_table, layer, n_pool)
    attn = attn.reshape(t, ATTN_WIDTH)
    u_t = u.reshape(bsz, nq, SSM_WIDTH).transpose(1, 0, 2)
    y_t, hr, hi = _s5_sample(u_t, h0r, h0i, lp['s5_sample'])
    y = y_t.transpose(1, 0, 2).reshape(t, SSM_WIDTH)
    x1, comb = _post_mixer(y, u, attn, x, lp, tm=256)
    x2 = _moe(x1, comb, lp, tm=t)
    return x2, k, v, hr, hi


def _layer_params(l, w_in, w_out, norm_attn_g, norm_ssm_g, ssm_a_re, ssm_a_im, ssm_log_dt,
                  ssm_b_re, ssm_b_im, ssm_c_re, ssm_c_im, ssm_d, w_glu, b_glu, ln1_g, ln1_b,
                  w_group, b_group, w_expert, b_expert, w_gate, w_up, w_down, ln2_g, ln2_b):
    rowv = lambda a: a[l].reshape(1, -1).astype(F32)
    s5_args = (ssm_a_re[l], ssm_a_im[l], ssm_log_dt[l], ssm_b_re[l], ssm_b_im[l], ssm_c_re[l], ssm_c_im[l])
    pad = LANES - N_GROUPS - N_EXPERTS
    w_router = jnp.concatenate([w_group[l], w_expert[l], jnp.zeros((D_MODEL, pad), F32)], axis=1)
    b_router = jnp.concatenate([b_group[l], b_expert[l], jnp.zeros((pad,), F32)]).reshape(1, LANES)
    return dict(
        w_in=w_in[l].astype(BF16), w_out=w_out[l].astype(BF16), w_glu=w_glu[l].astype(BF16),
        g_attn=rowv(norm_attn_g), g_ssm=rowv(norm_ssm_g), d=rowv(ssm_d), b_glu=rowv(b_glu),
        ln1_g=rowv(ln1_g), ln1_b=rowv(ln1_b), ln2_g=rowv(ln2_g), ln2_b=rowv(ln2_b),
        w_router=w_router.astype(F32), b_router=b_router.astype(F32),
        w_gate=w_gate[l].astype(BF16), w_up=w_up[l].astype(BF16), w_down=w_down[l].astype(BF16),
        s5_prompt=_s5_prompt_mats(*s5_args), s5_sample=_s5_sample_mats(*s5_args),
    )


def kernel(x_prompt, x_sample, cache_k, cache_v, state_ssm_re, state_ssm_im, page_table, w_in, w_out, norm_attn_g, norm_ssm_g, ssm_a_re, ssm_a_im, ssm_log_dt, ssm_b_re, ssm_b_im, ssm_c_re, ssm_c_im, ssm_d, w_glu, b_glu, ln1_g, ln1_b, w_group, b_group, w_expert, b_expert, w_gate, w_up, w_down, ln2_g, ln2_b):
    batch, seq_len, _ = x_prompt.shape
    dec_batch, dec_seq, _ = x_sample.shape
    depth, n_pool = cache_k.shape[0], cache_k.shape[1]
    n_pages = page_table.shape[1]
    past_len = n_pages * PAGE_SIZE
    assert seq_len % 512 == 0 and past_len % MOBA_BLOCK == 0 and dec_seq <= MOBA_BLOCK

    cos_p, sin_p = _rope_tables(jnp.arange(seq_len, dtype=jnp.int32))
    pos_s = past_len + jnp.arange(dec_seq, dtype=jnp.int32)
    cos_s, sin_s = _rope_tables(jnp.tile(pos_s, dec_batch))

    ck = cache_k.reshape(depth * n_pool, PAGE_SIZE, N_HEADS, HEAD_DIM)
    cv = cache_v.reshape(depth * n_pool, PAGE_SIZE, N_HEADS, HEAD_DIM)
    xp = x_prompt.reshape(batch * seq_len, D_MODEL)
    xs = x_sample.reshape(dec_batch * dec_seq, D_MODEL)
    kbuf = jnp.zeros((depth, batch * seq_len, N_HEADS, HEAD_DIM), F32)
    vbuf = jnp.zeros((depth, batch * seq_len, N_HEADS, HEAD_DIM), F32)
    outs = [[] for _ in range(6)]
    for l in range(depth):
        lp = _layer_params(l, w_in, w_out, norm_attn_g, norm_ssm_g, ssm_a_re, ssm_a_im, ssm_log_dt,
                           ssm_b_re, ssm_b_im, ssm_c_re, ssm_c_im, ssm_d, w_glu, b_glu, ln1_g, ln1_b,
                           w_group, b_group, w_expert, b_expert, w_gate, w_up, w_down, ln2_g, ln2_b)
        xp, kbuf, vbuf, hrp, hip = _prompt_layer(xp, lp, batch, seq_len, cos_p, sin_p, l, depth, kbuf, vbuf)
        h0r = state_ssm_re[l].reshape(dec_batch, SSM_GROUPS * SSM_STATE)
        h0i = state_ssm_im[l].reshape(dec_batch, SSM_GROUPS * SSM_STATE)
        xs, kn, vn, hrs, his = _sample_layer(xs, lp, dec_batch, dec_seq, cos_s, sin_s, ck, cv,
                                             page_table, l, n_pool, h0r, h0i)
        outs[0].append(hrp)
        outs[1].append(hip)
        outs[2].append(kn.reshape(dec_batch, dec_seq, N_HEADS, HEAD_DIM))
        outs[3].append(vn.reshape(dec_batch, dec_seq, N_HEADS, HEAD_DIM))
        outs[4].append(hrs.reshape(dec_batch, SSM_GROUPS, SSM_STATE))
        outs[5].append(his.reshape(dec_batch, SSM_GROUPS, SSM_STATE))
    kv_shape = (depth, batch, seq_len, N_HEADS, HEAD_DIM)
    return (xp.reshape(batch, seq_len, D_MODEL), xs.reshape(dec_batch, dec_seq, D_MODEL),
            kbuf.reshape(kv_shape), vbuf.reshape(kv_shape), *[jnp.stack(o) for o in outs])
```

```python
import functools
import math

import jax
import jax.numpy as jnp
from jax import lax
from jax.experimental import pallas as pl
from jax.experimental.pallas import tpu as pltpu

F32 = jnp.float32
BF16 = jnp.bfloat16
HIGHEST = lax.Precision.HIGHEST

D_MODEL = 1024
ATTN_WIDTH = 512
SSM_WIDTH = 512
HEAD_DIM = 64
N_HEADS = 8
MOBA_BLOCK = 256
MOBA_TOPK = 3
PAGE_SIZE = 128
ROPE_THETA = 10000.0
SSM_GROUP_DIM = 16
SSM_GROUPS = 32
SSM_STATE = 64
N_GROUPS = 4
EXPERTS_PER_GROUP = 4
N_EXPERTS = 16
D_EXPERT = 512
DEPTH = 4
ALPHA = (2.0 * DEPTH) ** 0.25
LN_EPS = 1e-5
RMS_EPS = 1e-6
NEG_INF = -1e30
LOG2E = 1.4426950408889634
PROJ_WIDTH = 3 * ATTN_WIDTH + SSM_WIDTH

LANES = 128
SSM_CHUNK = 32
POST_MIXER_SUB_ROWS = 256
VMEM_LIMIT = 48 * 1024 * 1024


def _params(*sem):
    return pltpu.CompilerParams(dimension_semantics=sem, vmem_limit_bytes=VMEM_LIMIT)


def _rope_tables(pos):
    inv_freq = 1.0 / jnp.power(ROPE_THETA, jnp.arange(0, HEAD_DIM, 2, dtype=F32) / HEAD_DIM)
    ang = pos.astype(F32)[:, None] * inv_freq[None, :]
    cos, sin = jnp.cos(ang), jnp.sin(ang)
    cos_t = jnp.concatenate([cos, cos, cos, cos], axis=-1)
    sin_t = jnp.concatenate([-sin, sin, -sin, sin], axis=-1)
    return cos_t, sin_t


def _rope(x, cos, sin):
    tm = x.shape[0]
    lane = lax.broadcasted_iota(jnp.int32, (tm, LANES), 1)
    first_half = (lane % HEAD_DIM) < (HEAD_DIM // 2)
    outs = []
    for c in range(x.shape[1] // LANES):
        xc = x[:, c * LANES:(c + 1) * LANES]
        partner = jnp.where(first_half, pltpu.roll(xc, LANES - HEAD_DIM // 2, 1),
                            pltpu.roll(xc, HEAD_DIM // 2, 1))
        outs.append(xc * cos + partner * sin)
    return jnp.concatenate(outs, axis=1)


def _inproj_prompt_kernel(x_ref, w_ref, cos_ref, sin_ref, *rest):
    k_ref, v_ref, u_ref, q_ref, kb_ref, vt_ref, km_ref = rest[-7:]
    tm = x_ref.shape[0]
    proj = jnp.dot(x_ref[...].astype(BF16), w_ref[...], preferred_element_type=F32)
    cos, sin = cos_ref[...], sin_ref[...]
    q = _rope(proj[:, :ATTN_WIDTH], cos, sin)
    k = _rope(proj[:, ATTN_WIDTH:2 * ATTN_WIDTH], cos, sin)
    v = proj[:, 2 * ATTN_WIDTH:3 * ATTN_WIDTH]
    u_ref[...] = proj[:, 3 * ATTN_WIDTH:]
    q_ref[...] = q * (HEAD_DIM ** -0.5 * LOG2E)
    kb_ref[...] = k.astype(BF16)
    vt = v.T
    k_ref[0, 0] = k.T.reshape(N_HEADS, HEAD_DIM, tm)
    v_ref[0, 0] = vt.reshape(N_HEADS, HEAD_DIM, tm)
    for blk in range(tm // MOBA_BLOCK):
        sl = slice(blk * MOBA_BLOCK, (blk + 1) * MOBA_BLOCK)
        vt_ref[:, blk] = vt[:, sl].reshape(N_HEADS, HEAD_DIM, MOBA_BLOCK).astype(BF16)
        km_ref[blk] = jnp.mean(k[sl], axis=0, keepdims=True)


def _inproj_prompt(x, w_in, cos_t, sin_t, seq_len, layer, depth, kbuf, vbuf, tm=512):
    t = x.shape[0]
    n_tiles_seq = seq_len // tm
    nblk = t // MOBA_BLOCK
    bpt = tm // MOBA_BLOCK
    row = lambda i: (i, 0)
    kv_sds = jax.ShapeDtypeStruct((depth, t // seq_len, N_HEADS, HEAD_DIM, seq_len), F32)
    out_shape = (
        kv_sds,
        kv_sds,
        jax.ShapeDtypeStruct((t, SSM_WIDTH), F32),
        jax.ShapeDtypeStruct((t, ATTN_WIDTH), F32),
        jax.ShapeDtypeStruct((t, ATTN_WIDTH), BF16),
        jax.ShapeDtypeStruct((N_HEADS, nblk, HEAD_DIM, MOBA_BLOCK), BF16),
        jax.ShapeDtypeStruct((nblk, 1, ATTN_WIDTH), F32),
    )
    kv_spec = pl.BlockSpec((1, 1, N_HEADS, HEAD_DIM, tm),
                           lambda i: (layer, i // n_tiles_seq, 0, 0, i % n_tiles_seq))
    any_spec = pl.BlockSpec(memory_space=pl.ANY)
    in_specs = [
        pl.BlockSpec((tm, D_MODEL), row),
        pl.BlockSpec((D_MODEL, PROJ_WIDTH), lambda i: (0, 0)),
        pl.BlockSpec((tm, LANES), lambda i: (i % n_tiles_seq, 0)),
        pl.BlockSpec((tm, LANES), lambda i: (i % n_tiles_seq, 0)),
    ]
    out_specs = (
        kv_spec, kv_spec,
        pl.BlockSpec((tm, SSM_WIDTH), row),
        pl.BlockSpec((tm, ATTN_WIDTH), row),
        pl.BlockSpec((tm, ATTN_WIDTH), row),
        pl.BlockSpec((N_HEADS, bpt, HEAD_DIM, MOBA_BLOCK), lambda i: (0, i, 0, 0)),
        pl.BlockSpec((bpt, 1, ATTN_WIDTH), lambda i: (i, 0, 0)),
    )
    return pl.pallas_call(
        _inproj_prompt_kernel,
        grid=(t // tm,),
        in_specs=in_specs + [any_spec, any_spec],
        out_specs=out_specs,
        out_shape=out_shape,
        input_output_aliases={4: 0, 5: 1},
        compiler_params=_params("parallel"),
        name="inproj_prompt",
    )(x, w_in, cos_t, sin_t, kbuf, vbuf)


def _inproj_sample_kernel(x_ref, w_ref, cos_ref, sin_ref, q_ref, k_ref, v_ref, u_ref):
    proj = jnp.dot(x_ref[...].astype(BF16), w_ref[...], preferred_element_type=F32)
    cos, sin = cos_ref[...], sin_ref[...]
    q_ref[...] = _rope(proj[:, :ATTN_WIDTH], cos, sin) * (HEAD_DIM ** -0.5)
    k_ref[...] = _rope(proj[:, ATTN_WIDTH:2 * ATTN_WIDTH], cos, sin)
    v_ref[...] = proj[:, 2 * ATTN_WIDTH:3 * ATTN_WIDTH]
    u_ref[...] = proj[:, 3 * ATTN_WIDTH:]


def _inproj_sample(x, w_in, cos_t, sin_t, tm=256):
    t = x.shape[0]
    tm = min(tm, t)
    row = lambda i: (i, 0)
    sds = jax.ShapeDtypeStruct((t, ATTN_WIDTH), F32)
    return pl.pallas_call(
        _inproj_sample_kernel,
        grid=(t // tm,),
        in_specs=[
            pl.BlockSpec((tm, D_MODEL), row),
            pl.BlockSpec((D_MODEL, PROJ_WIDTH), lambda i: (0, 0)),
            pl.BlockSpec((tm, LANES), row),
            pl.BlockSpec((tm, LANES), row),
        ],
        out_specs=tuple(pl.BlockSpec((tm, ATTN_WIDTH), row) for _ in range(4)),
        out_shape=(sds, sds, sds, sds),
        compiler_params=_params("parallel"),
        name="inproj_sample",
    )(x, w_in, cos_t, sin_t)


def _select_blocks(gate, n_valid, axis):
    nb = gate.shape[axis]
    blk = lax.broadcasted_iota(jnp.int32, gate.shape, axis)
    gate = jnp.where(blk < n_valid, gate, NEG_INF)
    sel = jnp.zeros(gate.shape, dtype=jnp.bool_)
    for r in range(MOBA_TOPK):
        top = jnp.max(gate, axis=axis, keepdims=True)
        first = jnp.min(jnp.where(gate == top, blk, nb), axis=axis, keepdims=True)
        hit = blk == first
        sel = jnp.logical_or(sel, jnp.logical_and(hit, r < n_valid))
        gate = jnp.where(hit, -jnp.inf, gate)
    return jnp.where(sel, 0.0, NEG_INF).astype(F32)


HEADS_PER_STEP = 4


def _moba_prompt_kernel(q_ref, k_ref, vt_ref, km_ref, o_ref, *scratch):
    hps = HEADS_PER_STEP
    qm_refs, bias_refs = scratch[:hps], scratch[hps:2 * hps]
    m_refs, l_refs, acc_refs = scratch[2 * hps:3 * hps], scratch[3 * hps:4 * hps], scratch[4 * hps:5 * hps]
    s_bufs = (scratch[5 * hps:6 * hps], scratch[6 * hps:7 * hps])
    qi = pl.program_id(2)
    q = q_ref[...]
    lane_head = lax.broadcasted_iota(jnp.int32, q.shape, 1) // HEAD_DIM
    km = km_ref[0]
    nt = (((1,), (1,)), ((), ()))
    for h in range(hps):
        qm = jnp.where(lane_head == h, q, 0.0)
        gate = lax.dot_general(km, qm, nt, precision=HIGHEST, preferred_element_type=F32)
        bias_refs[h][...] = _select_blocks(gate, qi, axis=0)
        qm_refs[h][...] = qm.astype(BF16)

    k_own = k_ref[pl.ds(pl.multiple_of(qi * MOBA_BLOCK, MOBA_BLOCK), MOBA_BLOCK), :]
    for h in range(hps):
        s = lax.dot_general(k_own, qm_refs[h][...], nt, preferred_element_type=F32)
        key_i = lax.broadcasted_iota(jnp.int32, s.shape, 0)
        qry_i = lax.broadcasted_iota(jnp.int32, s.shape, 1)
        s = jnp.where(key_i <= qry_i, s, NEG_INF)
        m0 = jnp.max(s, axis=0, keepdims=True)
        p = jnp.exp2(s - m0)
        m_refs[h][...] = m0
        l_refs[h][...] = jnp.sum(p, axis=0, keepdims=True)
        acc_refs[h][...] = jnp.dot(vt_ref[h, qi], p.astype(BF16), preferred_element_type=F32)

    last = jnp.maximum(qi - 1, 0)

    def issue_scores(j, bufs):
        j = jnp.minimum(j, last)
        k_j = k_ref[pl.ds(pl.multiple_of(j * MOBA_BLOCK, MOBA_BLOCK), MOBA_BLOCK), :]
        for h in range(hps):
            bufs[h][...] = lax.dot_general(k_j, qm_refs[h][...], nt, preferred_element_type=F32)

    def consume(j, bufs):
        for h in range(hps):
            s = bufs[h][...]
            bias = bias_refs[h][pl.ds(j, 1), :]
            m = m_refs[h][...]
            m_new = jnp.maximum(m, jnp.max(s, axis=0, keepdims=True) + bias)
            a = jnp.exp2(m - m_new)
            p = jnp.exp2(s + (bias - m_new))
            m_refs[h][...] = m_new
            l_refs[h][...] = a * l_refs[h][...] + jnp.sum(p, axis=0, keepdims=True)
            acc_refs[h][...] = a * acc_refs[h][...] + jnp.dot(vt_ref[h, j], p.astype(BF16),
                                                             preferred_element_type=F32)

    issue_scores(0, s_bufs[0])

    def pair(i, carry):
        j = 2 * i
        issue_scores(j + 1, s_bufs[1])
        consume(j, s_bufs[0])
        issue_scores(j + 2, s_bufs[0])
        consume(j + 1, s_bufs[1])
        return carry

    lax.fori_loop(0, qi // 2, pair, 0)

    @pl.when(qi % 2 == 1)
    def _():
        consume(qi - 1, s_bufs[0])

    out_t = jnp.concatenate([acc_refs[h][...] / l_refs[h][...] for h in range(hps)], axis=0)
    o_ref[...] = out_t.T


def _moba_prompt(q, kb, vt, km, batch, seq_len):
    t = q.shape[0]
    nb = seq_len // MOBA_BLOCK
    tq = MOBA_BLOCK
    hps = HEADS_PER_STEP
    gw = hps * HEAD_DIM
    scratch = ([pltpu.VMEM((tq, gw), BF16) for _ in range(hps)]
               + [pltpu.VMEM((nb, tq), F32) for _ in range(hps)]
               + [pltpu.VMEM((1, tq), F32) for _ in range(2 * hps)]
               + [pltpu.VMEM((HEAD_DIM, tq), F32) for _ in range(hps)]
               + [pltpu.VMEM((MOBA_BLOCK, tq), F32) for _ in range(2 * hps)])
    return pl.pallas_call(
        _moba_prompt_kernel,
        grid=(batch, N_HEADS // hps, nb),
        in_specs=[
            pl.BlockSpec((tq, gw), lambda b, g, i: (b * nb + i, g)),
            pl.BlockSpec((seq_len, gw), lambda b, g, i: (b, g)),
            pl.BlockSpec((hps, nb, HEAD_DIM, MOBA_BLOCK), lambda b, g, i: (g, b, 0, 0)),
            pl.BlockSpec((1, nb, gw), lambda b, g, i: (b, 0, g)),
        ],
        out_specs=pl.BlockSpec((tq, gw), lambda b, g, i: (b * nb + i, g)),
        out_shape=jax.ShapeDtypeStruct((t, ATTN_WIDTH), F32),
        scratch_shapes=scratch,
        compiler_params=_params("parallel", "parallel", "arbitrary"),
        name="moba_prompt",
    )(q, kb, vt, km)


def _moba_sample_kernel(n_pages, pt_ref, q_ref, kn_ref, vn_ref, *rest):
    del pt_ref
    k_pages = rest[:n_pages]
    v_pages = rest[n_pages:2 * n_pages]
    o_ref = rest[2 * n_pages]
    nq = q_ref.shape[1]
    rows = N_HEADS * nq
    ppb = MOBA_BLOCK // PAGE_SIZE
    nb_past = n_pages // ppb
    nt = (((1,), (1,)), ((), ()))

    q = q_ref[0]
    q_rep = jnp.concatenate([q] * N_HEADS, axis=0)
    r_i = lax.broadcasted_iota(jnp.int32, (rows, ATTN_WIDTH), 0)
    c_i = lax.broadcasted_iota(jnp.int32, (rows, ATTN_WIDTH), 1)
    head_mask = (c_i // HEAD_DIM) == (r_i // nq)
    qmat = jnp.where(head_mask, q_rep, 0.0)
    qmat_b = qmat.astype(BF16)

    def page_t(ref):
        return ref[0].reshape(ATTN_WIDTH, PAGE_SIZE)

    blk_lane = lax.broadcasted_iota(jnp.int32, (ATTN_WIDTH, nb_past), 1)
    kmean_t = jnp.zeros((ATTN_WIDTH, nb_past), F32)
    for n in range(nb_past):
        tot = page_t(k_pages[n * ppb])
        for pg in range(n * ppb + 1, (n + 1) * ppb):
            tot = tot + page_t(k_pages[pg])
        col = jnp.sum(tot, axis=1, keepdims=True) * (1.0 / MOBA_BLOCK)
        kmean_t = jnp.where(blk_lane == n, col, kmean_t)
    gate = jnp.dot(qmat, kmean_t, precision=HIGHEST, preferred_element_type=F32)
    bias = _select_blocks(gate, nb_past, axis=1)

    s_parts = []
    for pg in range(n_pages):
        sp = jnp.dot(qmat_b, page_t(k_pages[pg]).astype(BF16), preferred_element_type=F32)
        n = pg // ppb
        s_parts.append(sp + bias[:, n:n + 1])
    sn = lax.dot_general(qmat_b, kn_ref[0].astype(BF16), nt, preferred_element_type=F32)
    key_i = lax.broadcasted_iota(jnp.int32, (rows, nq), 1)
    qry_i = lax.broadcasted_iota(jnp.int32, (rows, nq), 0) % nq
    sn = jnp.where(key_i <= qry_i, sn, NEG_INF)

    m_el = s_parts[0]
    for sp in s_parts[1:]:
        m_el = jnp.maximum(m_el, sp)
    m = jnp.maximum(jnp.max(m_el, axis=1, keepdims=True), jnp.max(sn, axis=1, keepdims=True))
    pn = jnp.exp(sn - m)
    acc = jnp.dot(pn.astype(BF16), vn_ref[0].astype(BF16), preferred_element_type=F32)
    l_el = jnp.zeros((rows, PAGE_SIZE), F32)
    for pg in range(n_pages):
        p = jnp.exp(s_parts[pg] - m)
        l_el = l_el + p
        acc = acc + lax.dot_general(p.astype(BF16), page_t(v_pages[pg]).astype(BF16), nt,
                                    preferred_element_type=F32)
    l = jnp.sum(pn, axis=1, keepdims=True) + jnp.sum(l_el, axis=1, keepdims=True)
    out = jnp.where(head_mask, acc / l, 0.0)
    res = out[0:nq]
    for h in range(1, N_HEADS):
        res = res + out[h * nq:(h + 1) * nq]
    o_ref[0] = res


def _moba_sample(q, k_new, v_new, cache_k, cache_v, page_table, layer, n_pool):
    bsz, nq, _ = q.shape
    n_pages = page_table.shape[1]
    base = layer * n_pool
    tok = pl.BlockSpec((1, nq, ATTN_WIDTH), lambda b, pt: (b, 0, 0))

    def page_spec(pg):
        return pl.BlockSpec((1, N_HEADS, HEAD_DIM, PAGE_SIZE), lambda b, pt: (base + pt[b, pg], 0, 0, 0))

    grid_spec = pltpu.PrefetchScalarGridSpec(
        num_scalar_prefetch=1,
        grid=(bsz,),
        in_specs=[tok, tok, tok] + [page_spec(pg) for pg in range(n_pages)] * 2,
        out_specs=tok,
    )
    return pl.pallas_call(
        functools.partial(_moba_sample_kernel, n_pages),
        grid_spec=grid_spec,
        out_shape=jax.ShapeDtypeStruct((bsz, nq, ATTN_WIDTH), F32),
        compiler_params=_params("parallel"),
        name="moba_sample",
    )(page_table, q, k_new, v_new, *([cache_k] * n_pages), *([cache_v] * n_pages))


def _s5_discretise(a_re, a_im, log_dt, b_re, b_im):
    dt = jnp.exp(log_dt.astype(F32))[:, None]
    ar, ai = a_re.astype(F32), a_im.astype(F32)
    mag = jnp.exp(ar * dt)
    lr, li = mag * jnp.cos(ai * dt), mag * jnp.sin(ai * dt)
    den = ar * ar + ai * ai
    nr = lr - 1.0
    fr = (nr * ar + li * ai) / den
    fi = (li * ar - nr * ai) / den
    br_, bi_ = b_re.astype(F32), b_im.astype(F32)
    bbr = fr[..., None] * br_ - fi[..., None] * bi_
    bbi = fr[..., None] * bi_ + fi[..., None] * br_
    return ar * dt, ai * dt, lr, li, bbr, bbi


def _s5_prompt_mats(a_re, a_im, log_dt, b_re, b_im, c_re, c_im):
    c = SSM_CHUNK
    g_, p_, cdim = SSM_GROUPS, SSM_STATE, SSM_GROUP_DIM
    adr, adi, _, _, bbr, bbi = _s5_discretise(a_re, a_im, log_dt, b_re, b_im)
    tau = jnp.arange(c + 1, dtype=F32)[:, None, None]
    pmag = jnp.exp(adr[None] * tau)
    pw_r, pw_i = pmag * jnp.cos(adi[None] * tau), pmag * jnp.sin(adi[None] * tau)
    cr, ci = c_re.astype(F32), c_im.astype(F32)
    lb_r = pw_r[..., None] * bbr[None] - pw_i[..., None] * bbi[None]
    lb_i = pw_r[..., None] * bbi[None] + pw_i[..., None] * bbr[None]
    kern = (jnp.einsum('gop,tgpi->tgoi', cr, lb_r[:c], precision=HIGHEST)
            - jnp.einsum('gop,tgpi->tgoi', ci, lb_i[:c], precision=HIGHEST))
    s_i = jnp.arange(c)[:, None]
    t_i = jnp.arange(c)[None, :]
    lag = t_i - s_i
    toep = jnp.where((lag >= 0)[:, :, None, None, None], kern[jnp.clip(lag, 0, c - 1)], 0.0)
    toep = toep.transpose(2, 0, 4, 1, 3).reshape(g_, c * cdim, c * cdim)
    w_r = lb_r[:c][::-1].transpose(1, 0, 3, 2).reshape(g_, c * cdim, p_)
    w_i = lb_i[:c][::-1].transpose(1, 0, 3, 2).reshape(g_, c * cdim, p_)
    pr, pi = pw_r[1:], pw_i[1:]
    v_r = cr[None] * pr[:, :, None, :] - ci[None] * pi[:, :, None, :]
    v_i = -cr[None] * pi[:, :, None, :] - ci[None] * pr[:, :, None, :]
    v_r = v_r.transpose(1, 3, 0, 2).reshape(g_, p_, c * cdim)
    v_i = v_i.transpose(1, 3, 0, 2).reshape(g_, p_, c * cdim)
    decay = jnp.stack([pw_r[c], pw_i[c]], axis=1)
    return toep.astype(BF16), w_r.astype(BF16), w_i.astype(BF16), v_r.astype(BF16), v_i.astype(BF16), decay


def _s5_sample_mats(a_re, a_im, log_dt, b_re, b_im, c_re, c_im):
    g_, p_, cdim = SSM_GROUPS, SSM_STATE, SSM_GROUP_DIM
    _, _, lr, li, bbr, bbi = _s5_discretise(a_re, a_im, log_dt, b_re, b_im)
    eye = jnp.eye(g_, dtype=F32)
    b_bd_r = (eye[:, None, :, None] * bbr.transpose(0, 2, 1)[:, :, None, :]).reshape(g_ * cdim, g_ * p_)
    b_bd_i = (eye[:, None, :, None] * bbi.transpose(0, 2, 1)[:, :, None, :]).reshape(g_ * cdim, g_ * p_)
    cr, ci = c_re.astype(F32), c_im.astype(F32)
    c_bd_r = (eye[:, None, :, None] * cr.transpose(0, 2, 1)[:, :, None, :]).reshape(g_ * p_, g_ * cdim)
    c_bd_i = (eye[:, None, :, None] * ci.transpose(0, 2, 1)[:, :, None, :]).reshape(g_ * p_, g_ * cdim)
    return lr.reshape(1, g_ * p_), li.reshape(1, g_ * p_), b_bd_r, b_bd_i, c_bd_r, c_bd_i


def _s5_prompt_kernel(n_batch, u_ref, toep_ref, wr_ref, wi_ref, vr_ref, vi_ref, dec_ref,
                      y_ref, hr_ref, hi_ref, sr_s, si_s, pr_s, pi_s):
    u = u_ref[0]
    n_chunks = u.shape[0]
    cpb = n_chunks // n_batch
    sr_s[...] = jnp.dot(u, wr_ref[0], preferred_element_type=F32)
    si_s[...] = jnp.dot(u, wi_ref[0], preferred_element_type=F32)
    dr = dec_ref[0, 0:1, :]
    di = dec_ref[0, 1:2, :]

    def step(kk, carry):
        new = []
        for b in range(n_batch):
            hr, hi = carry[2 * b], carry[2 * b + 1]
            r = b * cpb + kk
            pr_s[pl.ds(r, 1), :] = hr
            pi_s[pl.ds(r, 1), :] = hi
            new.append(dr * hr - di * hi + sr_s[pl.ds(r, 1), :])
            new.append(dr * hi + di * hr + si_s[pl.ds(r, 1), :])
        return tuple(new)

    zero = jnp.zeros((1, SSM_STATE), F32)
    fin = lax.fori_loop(0, cpb, step, (zero,) * (2 * n_batch))
    for b in range(n_batch):
        hr_ref[0, b:b + 1, :] = fin[2 * b]
        hi_ref[0, b:b + 1, :] = fin[2 * b + 1]
    y = jnp.dot(u, toep_ref[0], preferred_element_type=F32)
    y = y + jnp.dot(pr_s[...].astype(BF16), vr_ref[0], preferred_element_type=F32)
    y = y + jnp.dot(pi_s[...].astype(BF16), vi_ref[0], preferred_element_type=F32)
    y_ref[0] = y


def _s5_prompt(u_g, mats, n_batch):
    toep, w_r, w_i, v_r, v_i, decay = mats
    g_, n_chunks, cw = u_g.shape
    per_g = lambda *shape: pl.BlockSpec((1,) + shape, lambda g: (g,) + (0,) * len(shape))
    st = jax.ShapeDtypeStruct((g_, n_batch, SSM_STATE), F32)
    return pl.pallas_call(
        functools.partial(_s5_prompt_kernel, n_batch),
        grid=(g_,),
        in_specs=[per_g(n_chunks, cw), per_g(cw, cw), per_g(cw, SSM_STATE), per_g(cw, SSM_STATE),
                  per_g(SSM_STATE, cw), per_g(SSM_STATE, cw), per_g(2, SSM_STATE)],
        out_specs=(per_g(n_chunks, cw), per_g(n_batch, SSM_STATE), per_g(n_batch, SSM_STATE)),
        out_shape=(jax.ShapeDtypeStruct((g_, n_chunks, cw), F32), st, st),
        scratch_shapes=[pltpu.VMEM((n_chunks, SSM_STATE), F32) for _ in range(4)],
        compiler_params=_params("parallel"),
        name="s5_prompt",
    )(u_g, toep, w_r, w_i, v_r, v_i, decay)


def _s5_sample_kernel(u_ref, h0r_ref, h0i_ref, lr_ref, li_ref, br_ref, bi_ref, cr_ref, ci_ref,
                      y_ref, hr_ref, hi_ref):
    n_steps = u_ref.shape[0]
    lr, li = lr_ref[...], li_ref[...]
    hr, hi = h0r_ref[...], h0i_ref[...]
    for t in range(n_steps):
        u = u_ref[t]
        bur = jnp.dot(u, br_ref[...], precision=HIGHEST, preferred_element_type=F32)
        bui = jnp.dot(u, bi_ref[...], precision=HIGHEST, preferred_element_type=F32)
        hr, hi = lr * hr - li * hi + bur, lr * hi + li * hr + bui
        y_ref[t] = (jnp.dot(hr, cr_ref[...], precision=HIGHEST, preferred_element_type=F32)
                    - jnp.dot(hi, ci_ref[...], precision=HIGHEST, preferred_element_type=F32))
    hr_ref[...] = hr
    hi_ref[...] = hi


def _s5_sample(u_t, h0r, h0i, mats):
    n_steps, bsz, _ = u_t.shape
    st = jax.ShapeDtypeStruct(h0r.shape, F32)
    return pl.pallas_call(
        _s5_sample_kernel,
        out_shape=(jax.ShapeDtypeStruct((n_steps, bsz, SSM_WIDTH), F32), st, st),
        compiler_params=pltpu.CompilerParams(vmem_limit_bytes=VMEM_LIMIT),
        name="s5_sample",
    )(u_t, h0r, h0i, *mats)


def _layer_norm(x, g, b):
    mu = jnp.mean(x, axis=-1, keepdims=True)
    xc = x - mu
    var = jnp.mean(xc * xc, axis=-1, keepdims=True)
    return xc * lax.rsqrt(var + LN_EPS) * g + b


def _rms_gain(y, g):
    return y * lax.rsqrt(jnp.mean(y * y, axis=-1, keepdims=True) + RMS_EPS) * g


def _sigmoid(x):
    return 1.0 / (1.0 + jnp.exp(-x))


def _gelu_tanh(x):
    return 0.5 * x * (1.0 + jnp.tanh(math.sqrt(2.0 / math.pi) * (x + 0.044715 * (x * x * x))))


def _route(lg, le):
    tm = lg.shape[0]
    gi = lax.broadcasted_iota(jnp.int32, lg.shape, 1)
    gmax = jnp.max(lg, axis=-1, keepdims=True)
    g_sel = jnp.min(jnp.where(lg == gmax, gi, N_GROUPS), axis=-1, keepdims=True)
    p_sel = 1.0 / jnp.sum(jnp.exp(lg - gmax), axis=-1, keepdims=True)
    ei = lax.broadcasted_iota(jnp.int32, (tm, N_EXPERTS), 1)
    cand = jnp.where((ei // EXPERTS_PER_GROUP) == g_sel, le, -jnp.inf)
    v0 = jnp.max(cand, axis=-1, keepdims=True)
    i0 = jnp.min(jnp.where(cand == v0, ei, N_EXPERTS), axis=-1, keepdims=True)
    cand = jnp.where(ei == i0, -jnp.inf, cand)
    v1 = jnp.max(cand, axis=-1, keepdims=True)
    i1 = jnp.min(jnp.where(cand == v1, ei, N_EXPERTS), axis=-1, keepdims=True)
    e1 = jnp.exp(v1 - v0)
    w0 = p_sel / (1.0 + e1)
    w1 = p_sel * e1 / (1.0 + e1)
    return jnp.where(ei == i0, w0, 0.0) + jnp.where(ei == i1, w1, 0.0)


def _post_mixer_kernel(y_ref, u_ref, a_ref, x_ref, d_ref, wglu_ref, bglu_ref, ga_ref, gs_ref,
                       wout_ref, g1_ref, b1_ref, wr_ref, br_ref, x1_ref, comb_ref):
    tm = x_ref.shape[0]
    sub = min(tm, POST_MIXER_SUB_ROWS)
    for r in range(tm // sub):
        rows = pl.ds(r * sub, sub)
        y = y_ref[rows, :] + d_ref[...] * u_ref[rows, :]
        g = _gelu_tanh(y)
        z = g * _sigmoid(jnp.dot(g.astype(BF16), wglu_ref[...], preferred_element_type=F32) + bglu_ref[...])
        mix_in = jnp.concatenate([_rms_gain(a_ref[rows, :], ga_ref[...]), _rms_gain(z, gs_ref[...])], axis=-1)
        mixed = jnp.dot(mix_in.astype(BF16), wout_ref[...], preferred_element_type=F32)
        x1 = _layer_norm(ALPHA * x_ref[rows, :] + mixed, g1_ref[...], b1_ref[...])
        x1_ref[rows, :] = x1
        logits = jnp.dot(x1, wr_ref[...], precision=HIGHEST, preferred_element_type=F32) + br_ref[...]
        comb_ref[rows, :] = _route(logits[:, :N_GROUPS], logits[:, N_GROUPS:N_GROUPS + N_EXPERTS])


def _post_mixer(y, u, attn, x, lp, tm):
    t = x.shape[0]
    tm = min(tm, t)
    row = lambda i: (i, 0)
    full = lambda a: pl.BlockSpec(a.shape, lambda i: (0, 0))
    weights = [lp['d'], lp['w_glu'], lp['b_glu'], lp['g_attn'], lp['g_ssm'], lp['w_out'],
               lp['ln1_g'], lp['ln1_b'], lp['w_router'], lp['b_router']]
    return pl.pallas_call(
        _post_mixer_kernel,
        grid=(t // tm,),
        in_specs=[pl.BlockSpec((tm, SSM_WIDTH), row), pl.BlockSpec((tm, SSM_WIDTH), row),
                  pl.BlockSpec((tm, ATTN_WIDTH), row), pl.BlockSpec((tm, D_MODEL), row)]
                 + [full(w) for w in weights],
        out_specs=(pl.BlockSpec((tm, D_MODEL), row), pl.BlockSpec((tm, N_EXPERTS), row)),
        out_shape=(jax.ShapeDtypeStruct((t, D_MODEL), F32), jax.ShapeDtypeStruct((t, N_EXPERTS), F32)),
        compiler_params=_params("parallel"),
        name="post_mixer",
    )(y, u, attn, x, *weights)


def _moe_kernel(x_ref, comb_ref, wg_ref, wu_ref, wd_ref, g2_ref, b2_ref, o_ref, acc_ref):
    e = pl.program_id(1)

    @pl.when(e == 0)
    def _():
        acc_ref[...] = jnp.zeros_like(acc_ref)

    xb = x_ref[...].astype(BF16)
    hg = jnp.dot(xb, wg_ref[0].astype(BF16), preferred_element_type=F32)
    hu = jnp.dot(xb, wu_ref[0].astype(BF16), preferred_element_type=F32)
    h = hg * _sigmoid(hg) * hu
    y = jnp.dot(h.astype(BF16), wd_ref[0].astype(BF16), preferred_element_type=F32)
    comb = comb_ref[...]
    ei = lax.broadcasted_iota(jnp.int32, comb.shape, 1)
    w = jnp.sum(jnp.where(ei == e, comb, 0.0), axis=-1, keepdims=True)
    acc_ref[...] += w * y

    @pl.when(e == pl.num_programs(1) - 1)
    def _():
        o_ref[...] = _layer_norm(ALPHA * x_ref[...] + acc_ref[...], g2_ref[...], b2_ref[...])


def _moe(x1, comb, lp, tm):
    t = x1.shape[0]
    layer = lp['layer']
    row = lambda i, e: (i, 0)
    return pl.pallas_call(
        _moe_kernel,
        grid=(t // tm, N_EXPERTS),
        in_specs=[
            pl.BlockSpec((tm, D_MODEL), row),
            pl.BlockSpec((tm, N_EXPERTS), row),
            pl.BlockSpec((None, 1, D_MODEL, D_EXPERT), lambda i, e: (layer, e, 0, 0)),
            pl.BlockSpec((None, 1, D_MODEL, D_EXPERT), lambda i, e: (layer, e, 0, 0)),
            pl.BlockSpec((None, 1, D_EXPERT, D_MODEL), lambda i, e: (layer, e, 0, 0)),
            pl.BlockSpec((1, D_MODEL), lambda i, e: (0, 0)),
            pl.BlockSpec((1, D_MODEL), lambda i, e: (0, 0)),
        ],
        out_specs=pl.BlockSpec((tm, D_MODEL), row),
        out_shape=jax.ShapeDtypeStruct((t, D_MODEL), F32),
        scratch_shapes=[pltpu.VMEM((tm, D_MODEL), F32)],
        compiler_params=_params("parallel", "arbitrary"),
        name="moe",
    )(x1, comb, lp['w_gate'], lp['w_up'], lp['w_down'], lp['ln2_g'], lp['ln2_b'])


def _prompt_layer(x, lp, batch, seq_len, cos_t, sin_t, layer, depth, kbuf, vbuf):
    t = x.shape[0]
    nb = seq_len // MOBA_BLOCK
    kbuf, vbuf, u, q, kb, vt, km = _inproj_prompt(x, lp['w_in'], cos_t, sin_t, seq_len,
                                                   layer, depth, kbuf, vbuf)
    attn = _moba_prompt(q, kb, vt, km.reshape(batch, nb, ATTN_WIDTH), batch, seq_len)
    c = SSM_CHUNK
    u_g = (u.astype(BF16).reshape(t // c, c, SSM_GROUPS, SSM_GROUP_DIM)
           .transpose(2, 0, 1, 3).reshape(SSM_GROUPS, t // c, c * SSM_GROUP_DIM))
    y_g, hr, hi = _s5_prompt(u_g, lp['s5_prompt'], batch)
    y = (y_g.reshape(SSM_GROUPS, t // c, c, SSM_GROUP_DIM)
         .transpose(1, 2, 0, 3).reshape(t, SSM_WIDTH))
    x1, comb = _post_mixer(y, u, attn, x, lp, tm=512)
    x2 = _moe(x1, comb, lp, tm=1024)
    return x2, kbuf, vbuf, hr.transpose(1, 0, 2), hi.transpose(1, 0, 2)


def _sample_layer(x, lp, bsz, nq, cos_t, sin_t, cache_k, cache_v, page_table, layer, n_pool,
                  h0r, h0i):
    t = x.shape[0]
    q, k, v, u = _inproj_sample(x, lp['w_in'], cos_t, sin_t)
    to3 = lambda a: a.reshape(bsz, nq, ATTN_WIDTH)
    attn = _moba_sample(to3(q), to3(k), to3(v), cache_k, cache_v, page_table, layer, n_pool)
    attn = attn.reshape(t, ATTN_WIDTH)
    u_t = u.reshape(bsz, nq, SSM_WIDTH).transpose(1, 0, 2)
    y_t, hr, hi = _s5_sample(u_t, h0r, h0i, lp['s5_sample'])
    y = y_t.transpose(1, 0, 2).reshape(t, SSM_WIDTH)
    x1, comb = _post_mixer(y, u, attn, x, lp, tm=256)
    x2 = _moe(x1, comb, lp, tm=t)
    return x2, k, v, hr, hi


def _layer_params(l, w_in, w_out, norm_attn_g, norm_ssm_g, ssm_a_re, ssm_a_im, ssm_log_dt,
                  ssm_b_re, ssm_b_im, ssm_c_re, ssm_c_im, ssm_d, w_glu, b_glu, ln1_g, ln1_b,
                  w_group, b_group, w_expert, b_expert, w_gate, w_up, w_down, ln2_g, ln2_b):
    rowv = lambda a: a[l].reshape(1, -1).astype(F32)
    s5_args = (ssm_a_re[l], ssm_a_im[l], ssm_log_dt[l], ssm_b_re[l], ssm_b_im[l], ssm_c_re[l], ssm_c_im[l])
    pad = LANES - N_GROUPS - N_EXPERTS
    w_router = jnp.concatenate([w_group[l], w_expert[l], jnp.zeros((D_MODEL, pad), F32)], axis=1)
    b_router = jnp.concatenate([b_group[l], b_expert[l], jnp.zeros((pad,), F32)]).reshape(1, LANES)
    return dict(
        w_in=w_in[l].astype(BF16), w_out=w_out[l].astype(BF16), w_glu=w_glu[l].astype(BF16),
        g_attn=rowv(norm_attn_g), g_ssm=rowv(norm_ssm_g), d=rowv(ssm_d), b_glu=rowv(b_glu),
        ln1_g=rowv(ln1_g), ln1_b=rowv(ln1_b), ln2_g=rowv(ln2_g), ln2_b=rowv(ln2_b),
        w_router=w_router.astype(F32), b_router=b_router.astype(F32),
        w_gate=w_gate, w_up=w_up, w_down=w_down, layer=l,
        s5_prompt=_s5_prompt_mats(*s5_args), s5_sample=_s5_sample_mats(*s5_args),
    )


def kernel(x_prompt, x_sample, cache_k, cache_v, state_ssm_re, state_ssm_im, page_table, w_in, w_out, norm_attn_g, norm_ssm_g, ssm_a_re, ssm_a_im, ssm_log_dt, ssm_b_re, ssm_b_im, ssm_c_re, ssm_c_im, ssm_d, w_glu, b_glu, ln1_g, ln1_b, w_group, b_group, w_expert, b_expert, w_gate, w_up, w_down, ln2_g, ln2_b):
    batch, seq_len, _ = x_prompt.shape
    dec_batch, dec_seq, _ = x_sample.shape
    depth, n_pool = cache_k.shape[0], cache_k.shape[1]
    n_pages = page_table.shape[1]
    past_len = n_pages * PAGE_SIZE
    assert seq_len % 512 == 0 and past_len % MOBA_BLOCK == 0 and dec_seq <= MOBA_BLOCK

    cos_p, sin_p = _rope_tables(jnp.arange(seq_len, dtype=jnp.int32))
    pos_s = past_len + jnp.arange(dec_seq, dtype=jnp.int32)
    cos_s, sin_s = _rope_tables(jnp.tile(pos_s, dec_batch))

    to_pages = lambda c: c.transpose(0, 1, 3, 4, 2).reshape(depth * n_pool, N_HEADS, HEAD_DIM, PAGE_SIZE)
    ck, cv = to_pages(cache_k), to_pages(cache_v)
    xp = x_prompt.reshape(batch * seq_len, D_MODEL)
    xs = x_sample.reshape(dec_batch * dec_seq, D_MODEL)
    kbuf = jnp.zeros((depth, batch, N_HEADS, HEAD_DIM, seq_len), F32)
    vbuf = jnp.zeros((depth, batch, N_HEADS, HEAD_DIM, seq_len), F32)
    outs = [[] for _ in range(6)]
    for l in range(depth):
        lp = _layer_params(l, w_in, w_out, norm_attn_g, norm_ssm_g, ssm_a_re, ssm_a_im, ssm_log_dt,
                           ssm_b_re, ssm_b_im, ssm_c_re, ssm_c_im, ssm_d, w_glu, b_glu, ln1_g, ln1_b,
                           w_group, b_group, w_expert, b_expert, w_gate, w_up, w_down, ln2_g, ln2_b)
        xp, kbuf, vbuf, hrp, hip = _prompt_layer(xp, lp, batch, seq_len, cos_p, sin_p, l, depth, kbuf, vbuf)
        h0r = state_ssm_re[l].reshape(dec_batch, SSM_GROUPS * SSM_STATE)
        h0i = state_ssm_im[l].reshape(dec_batch, SSM_GROUPS * SSM_STATE)
        xs, kn, vn, hrs, his = _sample_layer(xs, lp, dec_batch, dec_seq, cos_s, sin_s, ck, cv,
                                             page_table, l, n_pool, h0r, h0i)
        outs[0].append(hrp)
        outs[1].append(hip)
        outs[2].append(kn.reshape(dec_batch, dec_seq, N_HEADS, HEAD_DIM))
        outs[3].append(vn.reshape(dec_batch, dec_seq, N_HEADS, HEAD_DIM))
        outs[4].append(hrs.reshape(dec_batch, SSM_GROUPS, SSM_STATE))
        outs[5].append(his.reshape(dec_batch, SSM_GROUPS, SSM_STATE))
    return (xp.reshape(batch, seq_len, D_MODEL), xs.reshape(dec_batch, dec_seq, D_MODEL),
            kbuf.transpose(0, 1, 4, 2, 3), vbuf.transpose(0, 1, 4, 2, 3), *[jnp.stack(o) for o in outs])
```

```python
import functools
import math

import jax
import jax.numpy as jnp
from jax import lax
from jax.experimental import pallas as pl
from jax.experimental.pallas import tpu as pltpu

F32 = jnp.float32
BF16 = jnp.bfloat16
HIGHEST = lax.Precision.HIGHEST

D_MODEL = 1024
ATTN_WIDTH = 512
SSM_WIDTH = 512
HEAD_DIM = 64
N_HEADS = 8
MOBA_BLOCK = 256
MOBA_TOPK = 3
PAGE_SIZE = 128
ROPE_THETA = 10000.0
SSM_GROUP_DIM = 16
SSM_GROUPS = 32
SSM_STATE = 64
N_GROUPS = 4
EXPERTS_PER_GROUP = 4
N_EXPERTS = 16
D_EXPERT = 512
DEPTH = 4
ALPHA = (2.0 * DEPTH) ** 0.25
LN_EPS = 1e-5
RMS_EPS = 1e-6
NEG_INF = -1e30
LOG2E = 1.4426950408889634
PROJ_WIDTH = 3 * ATTN_WIDTH + SSM_WIDTH

LANES = 128
SSM_CHUNK = 32
VT_ROWS = HEAD_DIM + 16
POST_MIXER_SUB_ROWS = 256
VMEM_LIMIT = 48 * 1024 * 1024


def _params(*sem):
    return pltpu.CompilerParams(dimension_semantics=sem, vmem_limit_bytes=VMEM_LIMIT)


def _rope_tables(pos):
    inv_freq = 1.0 / jnp.power(ROPE_THETA, jnp.arange(0, HEAD_DIM, 2, dtype=F32) / HEAD_DIM)
    ang = pos.astype(F32)[:, None] * inv_freq[None, :]
    cos, sin = jnp.cos(ang), jnp.sin(ang)
    cos_t = jnp.concatenate([cos, cos, cos, cos], axis=-1)
    sin_t = jnp.concatenate([-sin, sin, -sin, sin], axis=-1)
    return cos_t, sin_t


def _rope(x, cos, sin):
    tm = x.shape[0]
    lane = lax.broadcasted_iota(jnp.int32, (tm, LANES), 1)
    first_half = (lane % HEAD_DIM) < (HEAD_DIM // 2)
    outs = []
    for c in range(x.shape[1] // LANES):
        xc = x[:, c * LANES:(c + 1) * LANES]
        partner = jnp.where(first_half, pltpu.roll(xc, LANES - HEAD_DIM // 2, 1),
                            pltpu.roll(xc, HEAD_DIM // 2, 1))
        outs.append(xc * cos + partner * sin)
    return jnp.concatenate(outs, axis=1)


def _inproj_prompt_kernel(x_ref, w_ref, cos_ref, sin_ref, *rest):
    k_ref, v_ref, u_ref, q_ref, kb_ref, vt_ref, km_ref = rest[-7:]
    tm = x_ref.shape[0]
    proj = jnp.dot(x_ref[...].astype(BF16), w_ref[...], preferred_element_type=F32)
    cos, sin = cos_ref[...], sin_ref[...]
    q = _rope(proj[:, :ATTN_WIDTH], cos, sin)
    k = _rope(proj[:, ATTN_WIDTH:2 * ATTN_WIDTH], cos, sin)
    v = proj[:, 2 * ATTN_WIDTH:3 * ATTN_WIDTH]
    u_ref[...] = proj[:, 3 * ATTN_WIDTH:]
    q_ref[...] = q * (HEAD_DIM ** -0.5 * LOG2E)
    kb_ref[...] = k.astype(BF16)
    vt = v.T
    k_ref[0, 0] = k.T.reshape(N_HEADS, HEAD_DIM, tm)
    v_ref[0, 0] = vt.reshape(N_HEADS, HEAD_DIM, tm)
    pad_row = lax.broadcasted_iota(jnp.int32, (N_HEADS, VT_ROWS - HEAD_DIM, MOBA_BLOCK), 1)
    ones_pad = jnp.where(pad_row == 0, 1.0, 0.0).astype(BF16)
    for blk in range(tm // MOBA_BLOCK):
        sl = slice(blk * MOBA_BLOCK, (blk + 1) * MOBA_BLOCK)
        vt_blk = vt[:, sl].reshape(N_HEADS, HEAD_DIM, MOBA_BLOCK).astype(BF16)
        vt_ref[:, blk] = jnp.concatenate([vt_blk, ones_pad], axis=1)
        km_ref[blk] = jnp.mean(k[sl], axis=0, keepdims=True)


def _inproj_prompt(x, w_in, cos_t, sin_t, seq_len, layer, depth, kbuf, vbuf, tm=512):
    t = x.shape[0]
    n_tiles_seq = seq_len // tm
    nblk = t // MOBA_BLOCK
    bpt = tm // MOBA_BLOCK
    row = lambda i: (i, 0)
    kv_sds = jax.ShapeDtypeStruct((depth, t // seq_len, N_HEADS, HEAD_DIM, seq_len), F32)
    out_shape = (
        kv_sds,
        kv_sds,
        jax.ShapeDtypeStruct((t, SSM_WIDTH), F32),
        jax.ShapeDtypeStruct((t, ATTN_WIDTH), F32),
        jax.ShapeDtypeStruct((t, ATTN_WIDTH), BF16),
        jax.ShapeDtypeStruct((N_HEADS, nblk, VT_ROWS, MOBA_BLOCK), BF16),
        jax.ShapeDtypeStruct((nblk, 1, ATTN_WIDTH), F32),
    )
    kv_spec = pl.BlockSpec((1, 1, N_HEADS, HEAD_DIM, tm),
                           lambda i: (layer, i // n_tiles_seq, 0, 0, i % n_tiles_seq))
    any_spec = pl.BlockSpec(memory_space=pl.ANY)
    in_specs = [
        pl.BlockSpec((tm, D_MODEL), row),
        pl.BlockSpec((D_MODEL, PROJ_WIDTH), lambda i: (0, 0)),
        pl.BlockSpec((tm, LANES), lambda i: (i % n_tiles_seq, 0)),
        pl.BlockSpec((tm, LANES), lambda i: (i % n_tiles_seq, 0)),
    ]
    out_specs = (
        kv_spec, kv_spec,
        pl.BlockSpec((tm, SSM_WIDTH), row),
        pl.BlockSpec((tm, ATTN_WIDTH), row),
        pl.BlockSpec((tm, ATTN_WIDTH), row),
        pl.BlockSpec((N_HEADS, bpt, VT_ROWS, MOBA_BLOCK), lambda i: (0, i, 0, 0)),
        pl.BlockSpec((bpt, 1, ATTN_WIDTH), lambda i: (i, 0, 0)),
    )
    return pl.pallas_call(
        _inproj_prompt_kernel,
        grid=(t // tm,),
        in_specs=in_specs + [any_spec, any_spec],
        out_specs=out_specs,
        out_shape=out_shape,
        input_output_aliases={4: 0, 5: 1},
        compiler_params=_params("parallel"),
        name="inproj_prompt",
    )(x, w_in, cos_t, sin_t, kbuf, vbuf)


def _inproj_sample_kernel(x_ref, w_ref, cos_ref, sin_ref, q_ref, k_ref, v_ref, u_ref):
    proj = jnp.dot(x_ref[...].astype(BF16), w_ref[...], preferred_element_type=F32)
    cos, sin = cos_ref[...], sin_ref[...]
    q_ref[...] = _rope(proj[:, :ATTN_WIDTH], cos, sin) * (HEAD_DIM ** -0.5)
    k_ref[...] = _rope(proj[:, ATTN_WIDTH:2 * ATTN_WIDTH], cos, sin)
    v_ref[...] = proj[:, 2 * ATTN_WIDTH:3 * ATTN_WIDTH]
    u_ref[...] = proj[:, 3 * ATTN_WIDTH:]


def _inproj_sample(x, w_in, cos_t, sin_t, tm=256):
    t = x.shape[0]
    tm = min(tm, t)
    row = lambda i: (i, 0)
    sds = jax.ShapeDtypeStruct((t, ATTN_WIDTH), F32)
    return pl.pallas_call(
        _inproj_sample_kernel,
        grid=(t // tm,),
        in_specs=[
            pl.BlockSpec((tm, D_MODEL), row),
            pl.BlockSpec((D_MODEL, PROJ_WIDTH), lambda i: (0, 0)),
            pl.BlockSpec((tm, LANES), row),
            pl.BlockSpec((tm, LANES), row),
        ],
        out_specs=tuple(pl.BlockSpec((tm, ATTN_WIDTH), row) for _ in range(4)),
        out_shape=(sds, sds, sds, sds),
        compiler_params=_params("parallel"),
        name="inproj_sample",
    )(x, w_in, cos_t, sin_t)


def _select_blocks(gate, n_valid, axis):
    nb = gate.shape[axis]
    blk = lax.broadcasted_iota(jnp.int32, gate.shape, axis)
    gate = jnp.where(blk < n_valid, gate, NEG_INF)
    sel = jnp.zeros(gate.shape, dtype=jnp.bool_)
    for r in range(MOBA_TOPK):
        top = jnp.max(gate, axis=axis, keepdims=True)
        first = jnp.min(jnp.where(gate == top, blk, nb), axis=axis, keepdims=True)
        hit = blk == first
        sel = jnp.logical_or(sel, jnp.logical_and(hit, r < n_valid))
        gate = jnp.where(hit, -jnp.inf, gate)
    return jnp.where(sel, 0.0, NEG_INF).astype(F32)


HEADS_PER_STEP = 4


def _moba_prompt_kernel(q_ref, k_ref, vt_ref, km_ref, o_ref, *scratch):
    hps = HEADS_PER_STEP
    qm_refs, bias_refs = scratch[:hps], scratch[hps:2 * hps]
    m_refs, acc_refs = scratch[2 * hps:3 * hps], scratch[3 * hps:4 * hps]
    s_bufs = (scratch[4 * hps:5 * hps], scratch[5 * hps:6 * hps])
    qi = pl.program_id(2)
    q = q_ref[...]
    lane_head = lax.broadcasted_iota(jnp.int32, q.shape, 1) // HEAD_DIM
    km = km_ref[0]
    nt = (((1,), (1,)), ((), ()))
    for h in range(hps):
        qm_refs[h][...] = jnp.where(lane_head == h, q, 0.0).astype(BF16)

    def issue_scores(j, bufs):
        k_j = k_ref[pl.ds(pl.multiple_of(j * MOBA_BLOCK, MOBA_BLOCK), MOBA_BLOCK), :]
        for h in range(hps):
            bufs[h][...] = lax.dot_general(k_j, qm_refs[h][...], nt, preferred_element_type=F32)

    def consume(j, bufs, own=False):
        for h in range(hps):
            s = bufs[h][...]
            if own:
                key_i = lax.broadcasted_iota(jnp.int32, s.shape, 0)
                qry_i = lax.broadcasted_iota(jnp.int32, s.shape, 1)
                s = jnp.where(key_i <= qry_i, s, NEG_INF)
                bias = jnp.zeros((1, s.shape[1]), F32)
            else:
                bias = bias_refs[h][pl.ds(j, 1), :]
            m = m_refs[h][...]
            m_new = jnp.maximum(m, jnp.max(s, axis=0, keepdims=True) + bias)
            a = jnp.exp2(m - m_new)
            p = jnp.exp2((s + (bias - m_new)).astype(BF16))
            m_refs[h][...] = m_new
            acc_refs[h][...] = a * acc_refs[h][...] + jnp.dot(vt_ref[h, j], p, preferred_element_type=F32)

    buf_b, buf_a = s_bufs
    issue_scores(qi, buf_b)
    for h in range(hps):
        qm = jnp.where(lane_head == h, q, 0.0)
        gate = lax.dot_general(km, qm, nt, precision=HIGHEST, preferred_element_type=F32)
        bias_refs[h][...] = _select_blocks(gate, qi, axis=0)
        m_refs[h][...] = jnp.full(m_refs[h].shape, NEG_INF, F32)
        acc_refs[h][...] = jnp.zeros(acc_refs[h].shape, F32)
    issue_scores(0, buf_a)
    consume(qi, buf_b, own=True)

    def quad(i, carry):
        j = 4 * i
        issue_scores(j + 1, buf_b)
        consume(j, buf_a)
        issue_scores(j + 2, buf_a)
        consume(j + 1, buf_b)
        issue_scores(j + 3, buf_b)
        consume(j + 2, buf_a)
        issue_scores(j + 4, buf_a)
        consume(j + 3, buf_b)
        return carry

    lax.fori_loop(0, qi // 4, quad, 0)
    j0 = (qi // 4) * 4
    rem = qi - j0

    @pl.when(rem >= 1)
    def _():
        issue_scores(j0 + 1, buf_b)
        consume(j0, buf_a)

    @pl.when(rem >= 2)
    def _():
        issue_scores(j0 + 2, buf_a)
        consume(j0 + 1, buf_b)

    @pl.when(rem == 3)
    def _():
        consume(j0 + 2, buf_a)

    out_t = jnp.concatenate(
        [acc_refs[h][:HEAD_DIM, :] / acc_refs[h][HEAD_DIM:HEAD_DIM + 1, :] for h in range(hps)], axis=0)
    o_ref[...] = out_t.T


def _moba_prompt(q, kb, vt, km, batch, seq_len):
    t = q.shape[0]
    nb = seq_len // MOBA_BLOCK
    tq = MOBA_BLOCK
    hps = HEADS_PER_STEP
    gw = hps * HEAD_DIM
    scratch = ([pltpu.VMEM((tq, gw), BF16) for _ in range(hps)]
               + [pltpu.VMEM((nb, tq), F32) for _ in range(hps)]
               + [pltpu.VMEM((1, tq), F32) for _ in range(hps)]
               + [pltpu.VMEM((VT_ROWS, tq), F32) for _ in range(hps)]
               + [pltpu.VMEM((MOBA_BLOCK, tq), F32) for _ in range(2 * hps)])
    return pl.pallas_call(
        _moba_prompt_kernel,
        grid=(batch, N_HEADS // hps, nb),
        in_specs=[
            pl.BlockSpec((tq, gw), lambda b, g, i: (b * nb + i, g)),
            pl.BlockSpec((seq_len, gw), lambda b, g, i: (b, g)),
            pl.BlockSpec((hps, nb, VT_ROWS, MOBA_BLOCK), lambda b, g, i: (g, b, 0, 0)),
            pl.BlockSpec((1, nb, gw), lambda b, g, i: (b, 0, g)),
        ],
        out_specs=pl.BlockSpec((tq, gw), lambda b, g, i: (b * nb + i, g)),
        out_shape=jax.ShapeDtypeStruct((t, ATTN_WIDTH), F32),
        scratch_shapes=scratch,
        compiler_params=_params("parallel", "parallel", "arbitrary"),
        name="moba_prompt",
    )(q, kb, vt, km)


def _moba_sample_kernel(n_pages, pt_ref, q_ref, kn_ref, vn_ref, *rest):
    del pt_ref
    k_pages = rest[:n_pages]
    v_pages = rest[n_pages:2 * n_pages]
    o_ref = rest[2 * n_pages]
    nq = q_ref.shape[1]
    rows = N_HEADS * nq
    ppb = MOBA_BLOCK // PAGE_SIZE
    nb_past = n_pages // ppb
    nt = (((1,), (1,)), ((), ()))

    q = q_ref[0]
    q_rep = jnp.concatenate([q] * N_HEADS, axis=0)
    r_i = lax.broadcasted_iota(jnp.int32, (rows, ATTN_WIDTH), 0)
    c_i = lax.broadcasted_iota(jnp.int32, (rows, ATTN_WIDTH), 1)
    head_mask = (c_i // HEAD_DIM) == (r_i // nq)
    qmat = jnp.where(head_mask, q_rep, 0.0)
    qmat_b = qmat.astype(BF16)

    def page_t(ref):
        return ref[0].reshape(ATTN_WIDTH, PAGE_SIZE)

    blk_lane = lax.broadcasted_iota(jnp.int32, (ATTN_WIDTH, nb_past), 1)
    kmean_t = jnp.zeros((ATTN_WIDTH, nb_past), F32)
    for n in range(nb_past):
        tot = page_t(k_pages[n * ppb])
        for pg in range(n * ppb + 1, (n + 1) * ppb):
            tot = tot + page_t(k_pages[pg])
        col = jnp.sum(tot, axis=1, keepdims=True) * (1.0 / MOBA_BLOCK)
        kmean_t = jnp.where(blk_lane == n, col, kmean_t)
    gate = jnp.dot(qmat, kmean_t, precision=HIGHEST, preferred_element_type=F32)
    bias = _select_blocks(gate, nb_past, axis=1)

    s_parts = []
    for pg in range(n_pages):
        sp = jnp.dot(qmat_b, page_t(k_pages[pg]).astype(BF16), preferred_element_type=F32)
        n = pg // ppb
        s_parts.append(sp + bias[:, n:n + 1])
    sn = lax.dot_general(qmat_b, kn_ref[0].astype(BF16), nt, preferred_element_type=F32)
    key_i = lax.broadcasted_iota(jnp.int32, (rows, nq), 1)
    qry_i = lax.broadcasted_iota(jnp.int32, (rows, nq), 0) % nq
    sn = jnp.where(key_i <= qry_i, sn, NEG_INF)

    m_el = s_parts[0]
    for sp in s_parts[1:]:
        m_el = jnp.maximum(m_el, sp)
    m = jnp.maximum(jnp.max(m_el, axis=1, keepdims=True), jnp.max(sn, axis=1, keepdims=True))
    pn = jnp.exp(sn - m)
    acc = jnp.dot(pn.astype(BF16), vn_ref[0].astype(BF16), preferred_element_type=F32)
    l_el = jnp.zeros((rows, PAGE_SIZE), F32)
    for pg in range(n_pages):
        p = jnp.exp(s_parts[pg] - m)
        l_el = l_el + p
        acc = acc + lax.dot_general(p.astype(BF16), page_t(v_pages[pg]).astype(BF16), nt,
                                    preferred_element_type=F32)
    l = jnp.sum(pn, axis=1, keepdims=True) + jnp.sum(l_el, axis=1, keepdims=True)
    out = jnp.where(head_mask, acc / l, 0.0)
    res = out[0:nq]
    for h in range(1, N_HEADS):
        res = res + out[h * nq:(h + 1) * nq]
    o_ref[0] = res


def _moba_sample(q, k_new, v_new, cache_k, cache_v, page_table, layer, n_pool):
    bsz, nq, _ = q.shape
    n_pages = page_table.shape[1]
    base = layer * n_pool
    tok = pl.BlockSpec((1, nq, ATTN_WIDTH), lambda b, pt: (b, 0, 0))

    def page_spec(pg):
        return pl.BlockSpec((1, N_HEADS, HEAD_DIM, PAGE_SIZE), lambda b, pt: (base + pt[b, pg], 0, 0, 0))

    grid_spec = pltpu.PrefetchScalarGridSpec(
        num_scalar_prefetch=1,
        grid=(bsz,),
        in_specs=[tok, tok, tok] + [page_spec(pg) for pg in range(n_pages)] * 2,
        out_specs=tok,
    )
    return pl.pallas_call(
        functools.partial(_moba_sample_kernel, n_pages),
        grid_spec=grid_spec,
        out_shape=jax.ShapeDtypeStruct((bsz, nq, ATTN_WIDTH), F32),
        compiler_params=_params("parallel"),
        name="moba_sample",
    )(page_table, q, k_new, v_new, *([cache_k] * n_pages), *([cache_v] * n_pages))


def _s5_discretise(a_re, a_im, log_dt, b_re, b_im):
    dt = jnp.exp(log_dt.astype(F32))[:, None]
    ar, ai = a_re.astype(F32), a_im.astype(F32)
    mag = jnp.exp(ar * dt)
    lr, li = mag * jnp.cos(ai * dt), mag * jnp.sin(ai * dt)
    den = ar * ar + ai * ai
    nr = lr - 1.0
    fr = (nr * ar + li * ai) / den
    fi = (li * ar - nr * ai) / den
    br_, bi_ = b_re.astype(F32), b_im.astype(F32)
    bbr = fr[..., None] * br_ - fi[..., None] * bi_
    bbi = fr[..., None] * bi_ + fi[..., None] * br_
    return ar * dt, ai * dt, lr, li, bbr, bbi


def _s5_prompt_mats(a_re, a_im, log_dt, b_re, b_im, c_re, c_im):
    c = SSM_CHUNK
    g_, p_, cdim = SSM_GROUPS, SSM_STATE, SSM_GROUP_DIM
    adr, adi, _, _, bbr, bbi = _s5_discretise(a_re, a_im, log_dt, b_re, b_im)
    tau = jnp.arange(c + 1, dtype=F32)[:, None, None]
    pmag = jnp.exp(adr[None] * tau)
    pw_r, pw_i = pmag * jnp.cos(adi[None] * tau), pmag * jnp.sin(adi[None] * tau)
    cr, ci = c_re.astype(F32), c_im.astype(F32)
    lb_r = pw_r[..., None] * bbr[None] - pw_i[..., None] * bbi[None]
    lb_i = pw_r[..., None] * bbi[None] + pw_i[..., None] * bbr[None]
    kern = (jnp.einsum('gop,tgpi->tgoi', cr, lb_r[:c], precision=HIGHEST)
            - jnp.einsum('gop,tgpi->tgoi', ci, lb_i[:c], precision=HIGHEST))
    s_i = jnp.arange(c)[:, None]
    t_i = jnp.arange(c)[None, :]
    lag = t_i - s_i
    toep = jnp.where((lag >= 0)[:, :, None, None, None], kern[jnp.clip(lag, 0, c - 1)], 0.0)
    toep = toep.transpose(2, 0, 4, 1, 3).reshape(g_, c * cdim, c * cdim)
    w_r = lb_r[:c][::-1].transpose(1, 0, 3, 2).reshape(g_, c * cdim, p_)
    w_i = lb_i[:c][::-1].transpose(1, 0, 3, 2).reshape(g_, c * cdim, p_)
    pr, pi = pw_r[1:], pw_i[1:]
    v_r = cr[None] * pr[:, :, None, :] - ci[None] * pi[:, :, None, :]
    v_i = -cr[None] * pi[:, :, None, :] - ci[None] * pr[:, :, None, :]
    v_r = v_r.transpose(1, 3, 0, 2).reshape(g_, p_, c * cdim)
    v_i = v_i.transpose(1, 3, 0, 2).reshape(g_, p_, c * cdim)
    decay = jnp.stack([pw_r[c], pw_i[c]], axis=1)
    return toep.astype(BF16), w_r.astype(BF16), w_i.astype(BF16), v_r.astype(BF16), v_i.astype(BF16), decay


def _s5_sample_mats(a_re, a_im, log_dt, b_re, b_im, c_re, c_im):
    g_, p_, cdim = SSM_GROUPS, SSM_STATE, SSM_GROUP_DIM
    _, _, lr, li, bbr, bbi = _s5_discretise(a_re, a_im, log_dt, b_re, b_im)
    eye = jnp.eye(g_, dtype=F32)
    b_bd_r = (eye[:, None, :, None] * bbr.transpose(0, 2, 1)[:, :, None, :]).reshape(g_ * cdim, g_ * p_)
    b_bd_i = (eye[:, None, :, None] * bbi.transpose(0, 2, 1)[:, :, None, :]).reshape(g_ * cdim, g_ * p_)
    cr, ci = c_re.astype(F32), c_im.astype(F32)
    c_bd_r = (eye[:, None, :, None] * cr.transpose(0, 2, 1)[:, :, None, :]).reshape(g_ * p_, g_ * cdim)
    c_bd_i = (eye[:, None, :, None] * ci.transpose(0, 2, 1)[:, :, None, :]).reshape(g_ * p_, g_ * cdim)
    return lr.reshape(1, g_ * p_), li.reshape(1, g_ * p_), b_bd_r, b_bd_i, c_bd_r, c_bd_i


def _s5_prompt_kernel(n_batch, u_ref, toep_ref, wr_ref, wi_ref, vr_ref, vi_ref, dec_ref,
                      y_ref, hr_ref, hi_ref, sr_s, si_s, pr_s, pi_s):
    u = u_ref[0]
    n_chunks = u.shape[0]
    cpb = n_chunks // n_batch
    sr_s[...] = jnp.dot(u, wr_ref[0], preferred_element_type=F32)
    si_s[...] = jnp.dot(u, wi_ref[0], preferred_element_type=F32)
    dr = dec_ref[0, 0:1, :]
    di = dec_ref[0, 1:2, :]

    def step(kk, carry):
        new = []
        for b in range(n_batch):
            hr, hi = carry[2 * b], carry[2 * b + 1]
            r = b * cpb + kk
            pr_s[pl.ds(r, 1), :] = hr
            pi_s[pl.ds(r, 1), :] = hi
            new.append(dr * hr - di * hi + sr_s[pl.ds(r, 1), :])
            new.append(dr * hi + di * hr + si_s[pl.ds(r, 1), :])
        return tuple(new)

    zero = jnp.zeros((1, SSM_STATE), F32)
    fin = lax.fori_loop(0, cpb, step, (zero,) * (2 * n_batch))
    for b in range(n_batch):
        hr_ref[0, b:b + 1, :] = fin[2 * b]
        hi_ref[0, b:b + 1, :] = fin[2 * b + 1]
    y = jnp.dot(u, toep_ref[0], preferred_element_type=F32)
    y = y + jnp.dot(pr_s[...].astype(BF16), vr_ref[0], preferred_element_type=F32)
    y = y + jnp.dot(pi_s[...].astype(BF16), vi_ref[0], preferred_element_type=F32)
    y_ref[0] = y


def _s5_prompt(u_g, mats, n_batch):
    toep, w_r, w_i, v_r, v_i, decay = mats
    g_, n_chunks, cw = u_g.shape
    per_g = lambda *shape: pl.BlockSpec((1,) + shape, lambda g: (g,) + (0,) * len(shape))
    st = jax.ShapeDtypeStruct((g_, n_batch, SSM_STATE), F32)
    return pl.pallas_call(
        functools.partial(_s5_prompt_kernel, n_batch),
        grid=(g_,),
        in_specs=[per_g(n_chunks, cw), per_g(cw, cw), per_g(cw, SSM_STATE), per_g(cw, SSM_STATE),
                  per_g(SSM_STATE, cw), per_g(SSM_STATE, cw), per_g(2, SSM_STATE)],
        out_specs=(per_g(n_chunks, cw), per_g(n_batch, SSM_STATE), per_g(n_batch, SSM_STATE)),
        out_shape=(jax.ShapeDtypeStruct((g_, n_chunks, cw), F32), st, st),
        scratch_shapes=[pltpu.VMEM((n_chunks, SSM_STATE), F32) for _ in range(4)],
        compiler_params=_params("parallel"),
        name="s5_prompt",
    )(u_g, toep, w_r, w_i, v_r, v_i, decay)


def _s5_sample_kernel(u_ref, h0r_ref, h0i_ref, lr_ref, li_ref, br_ref, bi_ref, cr_ref, ci_ref,
                      y_ref, hr_ref, hi_ref):
    n_steps = u_ref.shape[0]
    lr, li = lr_ref[...], li_ref[...]
    hr, hi = h0r_ref[...], h0i_ref[...]
    for t in range(n_steps):
        u = u_ref[t]
        bur = jnp.dot(u, br_ref[...], precision=HIGHEST, preferred_element_type=F32)
        bui = jnp.dot(u, bi_ref[...], precision=HIGHEST, preferred_element_type=F32)
        hr, hi = lr * hr - li * hi + bur, lr * hi + li * hr + bui
        y_ref[t] = (jnp.dot(hr, cr_ref[...], precision=HIGHEST, preferred_element_type=F32)
                    - jnp.dot(hi, ci_ref[...], precision=HIGHEST, preferred_element_type=F32))
    hr_ref[...] = hr
    hi_ref[...] = hi


def _s5_sample(u_t, h0r, h0i, mats):
    n_steps, bsz, _ = u_t.shape
    st = jax.ShapeDtypeStruct(h0r.shape, F32)
    return pl.pallas_call(
        _s5_sample_kernel,
        out_shape=(jax.ShapeDtypeStruct((n_steps, bsz, SSM_WIDTH), F32), st, st),
        compiler_params=pltpu.CompilerParams(vmem_limit_bytes=VMEM_LIMIT),
        name="s5_sample",
    )(u_t, h0r, h0i, *mats)


def _layer_norm(x, g, b):
    mu = jnp.mean(x, axis=-1, keepdims=True)
    xc = x - mu
    var = jnp.mean(xc * xc, axis=-1, keepdims=True)
    return xc * lax.rsqrt(var + LN_EPS) * g + b


def _rms_gain(y, g):
    return y * lax.rsqrt(jnp.mean(y * y, axis=-1, keepdims=True) + RMS_EPS) * g


def _sigmoid(x):
    return 1.0 / (1.0 + jnp.exp(-x))


def _gelu_tanh(x):
    return 0.5 * x * (1.0 + jnp.tanh(math.sqrt(2.0 / math.pi) * (x + 0.044715 * (x * x * x))))


def _route(lg, le):
    tm = lg.shape[0]
    gi = lax.broadcasted_iota(jnp.int32, lg.shape, 1)
    gmax = jnp.max(lg, axis=-1, keepdims=True)
    g_sel = jnp.min(jnp.where(lg == gmax, gi, N_GROUPS), axis=-1, keepdims=True)
    p_sel = 1.0 / jnp.sum(jnp.exp(lg - gmax), axis=-1, keepdims=True)
    ei = lax.broadcasted_iota(jnp.int32, (tm, N_EXPERTS), 1)
    cand = jnp.where((ei // EXPERTS_PER_GROUP) == g_sel, le, -jnp.inf)
    v0 = jnp.max(cand, axis=-1, keepdims=True)
    i0 = jnp.min(jnp.where(cand == v0, ei, N_EXPERTS), axis=-1, keepdims=True)
    cand = jnp.where(ei == i0, -jnp.inf, cand)
    v1 = jnp.max(cand, axis=-1, keepdims=True)
    i1 = jnp.min(jnp.where(cand == v1, ei, N_EXPERTS), axis=-1, keepdims=True)
    e1 = jnp.exp(v1 - v0)
    w0 = p_sel / (1.0 + e1)
    w1 = p_sel * e1 / (1.0 + e1)
    return jnp.where(ei == i0, w0, 0.0) + jnp.where(ei == i1, w1, 0.0)


def _post_mixer_kernel(y_ref, u_ref, a_ref, x_ref, d_ref, wglu_ref, bglu_ref, ga_ref, gs_ref,
                       wout_ref, g1_ref, b1_ref, wr_ref, br_ref, x1_ref, comb_ref):
    tm = x_ref.shape[0]
    sub = min(tm, POST_MIXER_SUB_ROWS)
    for r in range(tm // sub):
        rows = pl.ds(r * sub, sub)
        y = y_ref[rows, :] + d_ref[...] * u_ref[rows, :]
        g = _gelu_tanh(y)
        z = g * _sigmoid(jnp.dot(g.astype(BF16), wglu_ref[...], preferred_element_type=F32) + bglu_ref[...])
        mix_in = jnp.concatenate([_rms_gain(a_ref[rows, :], ga_ref[...]), _rms_gain(z, gs_ref[...])], axis=-1)
        mixed = jnp.dot(mix_in.astype(BF16), wout_ref[...], preferred_element_type=F32)
        x1 = _layer_norm(ALPHA * x_ref[rows, :] + mixed, g1_ref[...], b1_ref[...])
        x1_ref[rows, :] = x1
        logits = jnp.dot(x1, wr_ref[...], precision=HIGHEST, preferred_element_type=F32) + br_ref[...]
        comb_ref[rows, :] = _route(logits[:, :N_GROUPS], logits[:, N_GROUPS:N_GROUPS + N_EXPERTS])


def _post_mixer(y, u, attn, x, lp, tm):
    t = x.shape[0]
    tm = min(tm, t)
    row = lambda i: (i, 0)
    full = lambda a: pl.BlockSpec(a.shape, lambda i: (0, 0))
    weights = [lp['d'], lp['w_glu'], lp['b_glu'], lp['g_attn'], lp['g_ssm'], lp['w_out'],
               lp['ln1_g'], lp['ln1_b'], lp['w_router'], lp['b_router']]
    return pl.pallas_call(
        _post_mixer_kernel,
        grid=(t // tm,),
        in_specs=[pl.BlockSpec((tm, SSM_WIDTH), row), pl.BlockSpec((tm, SSM_WIDTH), row),
                  pl.BlockSpec((tm, ATTN_WIDTH), row), pl.BlockSpec((tm, D_MODEL), row)]
                 + [full(w) for w in weights],
        out_specs=(pl.BlockSpec((tm, D_MODEL), row), pl.BlockSpec((tm, N_EXPERTS), row)),
        out_shape=(jax.ShapeDtypeStruct((t, D_MODEL), F32), jax.ShapeDtypeStruct((t, N_EXPERTS), F32)),
        compiler_params=_params("parallel"),
        name="post_mixer",
    )(y, u, attn, x, *weights)


def _moe_kernel(x_ref, comb_ref, wg_ref, wu_ref, wd_ref, g2_ref, b2_ref, o_ref, acc_ref):
    e = pl.program_id(1)

    @pl.when(e == 0)
    def _():
        acc_ref[...] = jnp.zeros_like(acc_ref)

    xb = x_ref[...].astype(BF16)
    hg = jnp.dot(xb, wg_ref[0].astype(BF16), preferred_element_type=F32)
    hu = jnp.dot(xb, wu_ref[0].astype(BF16), preferred_element_type=F32)
    h = hg * _sigmoid(hg) * hu
    y = jnp.dot(h.astype(BF16), wd_ref[0].astype(BF16), preferred_element_type=F32)
    comb = comb_ref[...]
    ei = lax.broadcasted_iota(jnp.int32, comb.shape, 1)
    w = jnp.sum(jnp.where(ei == e, comb, 0.0), axis=-1, keepdims=True)
    acc_ref[...] += w * y

    @pl.when(e == pl.num_programs(1) - 1)
    def _():
        o_ref[...] = _layer_norm(ALPHA * x_ref[...] + acc_ref[...], g2_ref[...], b2_ref[...])


def _moe(x1, comb, lp, tm):
    t = x1.shape[0]
    layer = lp['layer']
    row = lambda i, e: (i, 0)
    return pl.pallas_call(
        _moe_kernel,
        grid=(t // tm, N_EXPERTS),
        in_specs=[
            pl.BlockSpec((tm, D_MODEL), row),
            pl.BlockSpec((tm, N_EXPERTS), row),
            pl.BlockSpec((None, 1, D_MODEL, D_EXPERT), lambda i, e: (layer, e, 0, 0)),
            pl.BlockSpec((None, 1, D_MODEL, D_EXPERT), lambda i, e: (layer, e, 0, 0)),
            pl.BlockSpec((None, 1, D_EXPERT, D_MODEL), lambda i, e: (layer, e, 0, 0)),
            pl.BlockSpec((1, D_MODEL), lambda i, e: (0, 0)),
            pl.BlockSpec((1, D_MODEL), lambda i, e: (0, 0)),
        ],
        out_specs=pl.BlockSpec((tm, D_MODEL), row),
        out_shape=jax.ShapeDtypeStruct((t, D_MODEL), F32),
        scratch_shapes=[pltpu.VMEM((tm, D_MODEL), F32)],
        compiler_params=_params("parallel", "arbitrary"),
        name="moe",
    )(x1, comb, lp['w_gate'], lp['w_up'], lp['w_down'], lp['ln2_g'], lp['ln2_b'])


def _prompt_layer(x, lp, batch, seq_len, cos_t, sin_t, layer, depth, kbuf, vbuf):
    t = x.shape[0]
    nb = seq_len // MOBA_BLOCK
    kbuf, vbuf, u, q, kb, vt, km = _inproj_prompt(x, lp['w_in'], cos_t, sin_t, seq_len,
                                                   layer, depth, kbuf, vbuf)
    attn = _moba_prompt(q, kb, vt, km.reshape(batch, nb, ATTN_WIDTH), batch, seq_len)
    c = SSM_CHUNK
    u_g = (u.astype(BF16).reshape(t // c, c, SSM_GROUPS, SSM_GROUP_DIM)
           .transpose(2, 0, 1, 3).reshape(SSM_GROUPS, t // c, c * SSM_GROUP_DIM))
    y_g, hr, hi = _s5_prompt(u_g, lp['s5_prompt'], batch)
    y = (y_g.reshape(SSM_GROUPS, t // c, c, SSM_GROUP_DIM)
         .transpose(1, 2, 0, 3).reshape(t, SSM_WIDTH))
    x1, comb = _post_mixer(y, u, attn, x, lp, tm=512)
    x2 = _moe(x1, comb, lp, tm=1024)
    return x2, kbuf, vbuf, hr.transpose(1, 0, 2), hi.transpose(1, 0, 2)


def _sample_layer(x, lp, bsz, nq, cos_t, sin_t, cache_k, cache_v, page_table, layer, n_pool,
                  h0r, h0i):
    t = x.shape[0]
    q, k, v, u = _inproj_sample(x, lp['w_in'], cos_t, sin_t)
    to3 = lambda a: a.reshape(bsz, nq, ATTN_WIDTH)
    attn = _moba_sample(to3(q), to3(k), to3(v), cache_k, cache_v, page_table, layer, n_pool)
    attn = attn.reshape(t, ATTN_WIDTH)
    u_t = u.reshape(bsz, nq, SSM_WIDTH).transpose(1, 0, 2)
    y_t, hr, hi = _s5_sample(u_t, h0r, h0i, lp['s5_sample'])
    y = y_t.transpose(1, 0, 2).reshape(t, SSM_WIDTH)
    x1, comb = _post_mixer(y, u, attn, x, lp, tm=256)
    x2 = _moe(x1, comb, lp, tm=t)
    return x2, k, v, hr, hi


def _all_layer_params(w_in, w_out, norm_attn_g, norm_ssm_g, ssm_a_re, ssm_a_im, ssm_log_dt,
                      ssm_b_re, ssm_b_im, ssm_c_re, ssm_c_im, ssm_d, w_glu, b_glu, ln1_g, ln1_b,
                      w_group, b_group, w_expert, b_expert, w_gate, w_up, w_down, ln2_g, ln2_b):
    depth = w_in.shape[0]
    rowv = lambda a: a.reshape(depth, 1, -1).astype(F32)
    s5_args = (ssm_a_re, ssm_a_im, ssm_log_dt, ssm_b_re, ssm_b_im, ssm_c_re, ssm_c_im)
    pad = LANES - N_GROUPS - N_EXPERTS
    w_router = jnp.concatenate([w_group, w_expert, jnp.zeros((depth, D_MODEL, pad), F32)], axis=2)
    b_router = jnp.concatenate([b_group, b_expert, jnp.zeros((depth, pad), F32)], axis=1)
    return dict(
        w_in=w_in.astype(BF16), w_out=w_out.astype(BF16), w_glu=w_glu.astype(BF16),
        g_attn=rowv(norm_attn_g), g_ssm=rowv(norm_ssm_g), d=rowv(ssm_d), b_glu=rowv(b_glu),
        ln1_g=rowv(ln1_g), ln1_b=rowv(ln1_b), ln2_g=rowv(ln2_g), ln2_b=rowv(ln2_b),
        w_router=w_router.astype(F32), b_router=rowv(b_router),
        s5_prompt=jax.vmap(_s5_prompt_mats)(*s5_args), s5_sample=jax.vmap(_s5_sample_mats)(*s5_args),
    ), dict(w_gate=w_gate, w_up=w_up, w_down=w_down)


def _layer_params(stacked, experts, l):
    lp = jax.tree.map(lambda a: a[l], stacked)
    return dict(lp, layer=l, **experts)


def kernel(x_prompt, x_sample, cache_k, cache_v, state_ssm_re, state_ssm_im, page_table, w_in, w_out, norm_attn_g, norm_ssm_g, ssm_a_re, ssm_a_im, ssm_log_dt, ssm_b_re, ssm_b_im, ssm_c_re, ssm_c_im, ssm_d, w_glu, b_glu, ln1_g, ln1_b, w_group, b_group, w_expert, b_expert, w_gate, w_up, w_down, ln2_g, ln2_b):
    batch, seq_len, _ = x_prompt.shape
    dec_batch, dec_seq, _ = x_sample.shape
    depth, n_pool = cache_k.shape[0], cache_k.shape[1]
    n_pages = page_table.shape[1]
    past_len = n_pages * PAGE_SIZE
    assert seq_len % 512 == 0 and past_len % MOBA_BLOCK == 0 and dec_seq <= MOBA_BLOCK

    cos_p, sin_p = _rope_tables(jnp.arange(seq_len, dtype=jnp.int32))
    pos_s = past_len + jnp.arange(dec_seq, dtype=jnp.int32)
    cos_s, sin_s = _rope_tables(jnp.tile(pos_s, dec_batch))

    to_pages = lambda c: c.transpose(0, 1, 3, 4, 2).reshape(depth * n_pool, N_HEADS, HEAD_DIM, PAGE_SIZE)
    ck, cv = to_pages(cache_k), to_pages(cache_v)
    xp = x_prompt.reshape(batch * seq_len, D_MODEL)
    xs = x_sample.reshape(dec_batch * dec_seq, D_MODEL)
    kbuf = jnp.zeros((depth, batch, N_HEADS, HEAD_DIM, seq_len), F32)
    vbuf = jnp.zeros((depth, batch, N_HEADS, HEAD_DIM, seq_len), F32)
    outs = [[] for _ in range(6)]
    stacked, experts = _all_layer_params(
        w_in, w_out, norm_attn_g, norm_ssm_g, ssm_a_re, ssm_a_im, ssm_log_dt, ssm_b_re, ssm_b_im,
        ssm_c_re, ssm_c_im, ssm_d, w_glu, b_glu, ln1_g, ln1_b, w_group, b_group, w_expert, b_expert,
        w_gate, w_up, w_down, ln2_g, ln2_b)
    for l in range(depth):
        lp = _layer_params(stacked, experts, l)
        xp, kbuf, vbuf, hrp, hip = _prompt_layer(xp, lp, batch, seq_len, cos_p, sin_p, l, depth, kbuf, vbuf)
        h0r = state_ssm_re[l].reshape(dec_batch, SSM_GROUPS * SSM_STATE)
        h0i = state_ssm_im[l].reshape(dec_batch, SSM_GROUPS * SSM_STATE)
        xs, kn, vn, hrs, his = _sample_layer(xs, lp, dec_batch, dec_seq, cos_s, sin_s, ck, cv,
                                             page_table, l, n_pool, h0r, h0i)
        outs[0].append(hrp)
        outs[1].append(hip)
        outs[2].append(kn.reshape(dec_batch, dec_seq, N_HEADS, HEAD_DIM))
        outs[3].append(vn.reshape(dec_batch, dec_seq, N_HEADS, HEAD_DIM))
        outs[4].append(hrs.reshape(dec_batch, SSM_GROUPS, SSM_STATE))
        outs[5].append(his.reshape(dec_batch, SSM_GROUPS, SSM_STATE))
    return (xp.reshape(batch, seq_len, D_MODEL), xs.reshape(dec_batch, dec_seq, D_MODEL),
            kbuf.transpose(0, 1, 4, 2, 3), vbuf.transpose(0, 1, 4, 2, 3), *[jnp.stack(o) for o in outs])
```

```python
import functools
import math

import jax
import jax.numpy as jnp
from jax import lax
from jax.experimental import pallas as pl
from jax.experimental.pallas import tpu as pltpu

F32 = jnp.float32
BF16 = jnp.bfloat16
HIGHEST = lax.Precision.HIGHEST

D_MODEL = 1024
ATTN_WIDTH = 512
SSM_WIDTH = 512
HEAD_DIM = 64
N_HEADS = 8
MOBA_BLOCK = 256
MOBA_TOPK = 3
PAGE_SIZE = 128
ROPE_THETA = 10000.0
SSM_GROUP_DIM = 16
SSM_GROUPS = 32
SSM_STATE = 64
N_GROUPS = 4
EXPERTS_PER_GROUP = 4
N_EXPERTS = 16
D_EXPERT = 512
DEPTH = 4
ALPHA = (2.0 * DEPTH) ** 0.25
LN_EPS = 1e-5
RMS_EPS = 1e-6
NEG_INF = -1e30
LOG2E = 1.4426950408889634
PROJ_WIDTH = 3 * ATTN_WIDTH + SSM_WIDTH

LANES = 128
SSM_CHUNK = 16
VT_ROWS = HEAD_DIM + 16
POST_MIXER_SUB_ROWS = 256
VMEM_LIMIT = 48 * 1024 * 1024


def _params(*sem):
    return pltpu.CompilerParams(dimension_semantics=sem, vmem_limit_bytes=VMEM_LIMIT)


def _rope_tables(pos):
    inv_freq = 1.0 / jnp.power(ROPE_THETA, jnp.arange(0, HEAD_DIM, 2, dtype=F32) / HEAD_DIM)
    ang = pos.astype(F32)[:, None] * inv_freq[None, :]
    cos, sin = jnp.cos(ang), jnp.sin(ang)
    cos_t = jnp.concatenate([cos, cos, cos, cos], axis=-1)
    sin_t = jnp.concatenate([-sin, sin, -sin, sin], axis=-1)
    return cos_t, sin_t


def _rope(x, cos, sin):
    tm = x.shape[0]
    lane = lax.broadcasted_iota(jnp.int32, (tm, LANES), 1)
    first_half = (lane % HEAD_DIM) < (HEAD_DIM // 2)
    outs = []
    for c in range(x.shape[1] // LANES):
        xc = x[:, c * LANES:(c + 1) * LANES]
        partner = jnp.where(first_half, pltpu.roll(xc, LANES - HEAD_DIM // 2, 1),
                            pltpu.roll(xc, HEAD_DIM // 2, 1))
        outs.append(xc * cos + partner * sin)
    return jnp.concatenate(outs, axis=1)


def _inproj_prompt_kernel(x_ref, w_ref, cos_ref, sin_ref, *rest):
    k_ref, v_ref, u_ref, q_ref, kb_ref, vt_ref, km_ref = rest[-7:]
    tm = x_ref.shape[0]
    proj = jnp.dot(x_ref[...].astype(BF16), w_ref[...], preferred_element_type=F32)
    cos, sin = cos_ref[...], sin_ref[...]
    q = _rope(proj[:, :ATTN_WIDTH], cos, sin)
    k = _rope(proj[:, ATTN_WIDTH:2 * ATTN_WIDTH], cos, sin)
    v = proj[:, 2 * ATTN_WIDTH:3 * ATTN_WIDTH]
    u_ref[...] = proj[:, 3 * ATTN_WIDTH:]
    q_ref[...] = q * (HEAD_DIM ** -0.5 * LOG2E)
    kb_ref[...] = k.astype(BF16)
    vt = v.T
    k_ref[0, 0] = k.T.reshape(N_HEADS, HEAD_DIM, tm)
    v_ref[0, 0] = vt.reshape(N_HEADS, HEAD_DIM, tm)
    pad_row = lax.broadcasted_iota(jnp.int32, (N_HEADS, VT_ROWS - HEAD_DIM, MOBA_BLOCK), 1)
    ones_pad = jnp.where(pad_row == 0, 1.0, 0.0).astype(BF16)
    for blk in range(tm // MOBA_BLOCK):
        sl = slice(blk * MOBA_BLOCK, (blk + 1) * MOBA_BLOCK)
        vt_blk = vt[:, sl].reshape(N_HEADS, HEAD_DIM, MOBA_BLOCK).astype(BF16)
        vt_ref[:, blk] = jnp.concatenate([vt_blk, ones_pad], axis=1)
        km_ref[blk] = jnp.mean(k[sl], axis=0, keepdims=True)


def _inproj_prompt(x, w_in, cos_t, sin_t, seq_len, layer, depth, kbuf, vbuf, tm=512):
    t = x.shape[0]
    n_tiles_seq = seq_len // tm
    nblk = t // MOBA_BLOCK
    bpt = tm // MOBA_BLOCK
    row = lambda i: (i, 0)
    kv_sds = jax.ShapeDtypeStruct((depth, t // seq_len, N_HEADS, HEAD_DIM, seq_len), F32)
    out_shape = (
        kv_sds,
        kv_sds,
        jax.ShapeDtypeStruct((t, SSM_WIDTH), F32),
        jax.ShapeDtypeStruct((t, ATTN_WIDTH), F32),
        jax.ShapeDtypeStruct((t, ATTN_WIDTH), BF16),
        jax.ShapeDtypeStruct((N_HEADS, nblk, VT_ROWS, MOBA_BLOCK), BF16),
        jax.ShapeDtypeStruct((nblk, 1, ATTN_WIDTH), F32),
    )
    kv_spec = pl.BlockSpec((1, 1, N_HEADS, HEAD_DIM, tm),
                           lambda i: (layer, i // n_tiles_seq, 0, 0, i % n_tiles_seq))
    any_spec = pl.BlockSpec(memory_space=pl.ANY)
    in_specs = [
        pl.BlockSpec((tm, D_MODEL), row),
        pl.BlockSpec((D_MODEL, PROJ_WIDTH), lambda i: (0, 0)),
        pl.BlockSpec((tm, LANES), lambda i: (i % n_tiles_seq, 0)),
        pl.BlockSpec((tm, LANES), lambda i: (i % n_tiles_seq, 0)),
    ]
    out_specs = (
        kv_spec, kv_spec,
        pl.BlockSpec((tm, SSM_WIDTH), row),
        pl.BlockSpec((tm, ATTN_WIDTH), row),
        pl.BlockSpec((tm, ATTN_WIDTH), row),
        pl.BlockSpec((N_HEADS, bpt, VT_ROWS, MOBA_BLOCK), lambda i: (0, i, 0, 0)),
        pl.BlockSpec((bpt, 1, ATTN_WIDTH), lambda i: (i, 0, 0)),
    )
    return pl.pallas_call(
        _inproj_prompt_kernel,
        grid=(t // tm,),
        in_specs=in_specs + [any_spec, any_spec],
        out_specs=out_specs,
        out_shape=out_shape,
        input_output_aliases={4: 0, 5: 1},
        compiler_params=_params("parallel"),
        name="inproj_prompt",
    )(x, w_in, cos_t, sin_t, kbuf, vbuf)


def _inproj_sample_kernel(x_ref, w_ref, cos_ref, sin_ref, q_ref, k_ref, v_ref, u_ref):
    proj = jnp.dot(x_ref[...].astype(BF16), w_ref[...], preferred_element_type=F32)
    cos, sin = cos_ref[...], sin_ref[...]
    q_ref[...] = _rope(proj[:, :ATTN_WIDTH], cos, sin) * (HEAD_DIM ** -0.5)
    k_ref[...] = _rope(proj[:, ATTN_WIDTH:2 * ATTN_WIDTH], cos, sin)
    v_ref[...] = proj[:, 2 * ATTN_WIDTH:3 * ATTN_WIDTH]
    u_ref[...] = proj[:, 3 * ATTN_WIDTH:]


def _inproj_sample(x, w_in, cos_t, sin_t, tm=256):
    t = x.shape[0]
    tm = min(tm, t)
    row = lambda i: (i, 0)
    sds = jax.ShapeDtypeStruct((t, ATTN_WIDTH), F32)
    return pl.pallas_call(
        _inproj_sample_kernel,
        grid=(t // tm,),
        in_specs=[
            pl.BlockSpec((tm, D_MODEL), row),
            pl.BlockSpec((D_MODEL, PROJ_WIDTH), lambda i: (0, 0)),
            pl.BlockSpec((tm, LANES), row),
            pl.BlockSpec((tm, LANES), row),
        ],
        out_specs=tuple(pl.BlockSpec((tm, ATTN_WIDTH), row) for _ in range(4)),
        out_shape=(sds, sds, sds, sds),
        compiler_params=_params("parallel"),
        name="inproj_sample",
    )(x, w_in, cos_t, sin_t)


def _select_blocks(gate, n_valid, axis):
    nb = gate.shape[axis]
    blk = lax.broadcasted_iota(jnp.int32, gate.shape, axis)
    gate = jnp.where(blk < n_valid, gate, NEG_INF)
    sel = jnp.zeros(gate.shape, dtype=jnp.bool_)
    for r in range(MOBA_TOPK):
        top = jnp.max(gate, axis=axis, keepdims=True)
        first = jnp.min(jnp.where(gate == top, blk, nb), axis=axis, keepdims=True)
        hit = blk == first
        sel = jnp.logical_or(sel, jnp.logical_and(hit, r < n_valid))
        gate = jnp.where(hit, -jnp.inf, gate)
    return jnp.where(sel, 0.0, NEG_INF).astype(F32)


HEADS_PER_STEP = 4


def _moba_prompt_kernel(q_ref, k_ref, vt_ref, km_ref, o_ref, *scratch):
    hps = HEADS_PER_STEP
    qm_refs, bias_refs = scratch[:hps], scratch[hps:2 * hps]
    m_refs, acc_refs = scratch[2 * hps:3 * hps], scratch[3 * hps:4 * hps]
    s_bufs = (scratch[4 * hps:5 * hps], scratch[5 * hps:6 * hps])
    qi = pl.program_id(2)
    q = q_ref[...]
    lane_head = lax.broadcasted_iota(jnp.int32, q.shape, 1) // HEAD_DIM
    km = km_ref[0]
    nt = (((1,), (1,)), ((), ()))
    for h in range(hps):
        qm_refs[h][...] = jnp.where(lane_head == h, q, 0.0).astype(BF16)

    def issue_scores(j, bufs):
        k_j = k_ref[pl.ds(pl.multiple_of(j * MOBA_BLOCK, MOBA_BLOCK), MOBA_BLOCK), :]
        for h in range(hps):
            bufs[h][...] = lax.dot_general(k_j, qm_refs[h][...], nt, preferred_element_type=F32)

    def consume(j, bufs, own=False):
        for h in range(hps):
            s = bufs[h][...]
            if own:
                key_i = lax.broadcasted_iota(jnp.int32, s.shape, 0)
                qry_i = lax.broadcasted_iota(jnp.int32, s.shape, 1)
                s = jnp.where(key_i <= qry_i, s, NEG_INF)
                bias = jnp.zeros((1, s.shape[1]), F32)
            else:
                bias = bias_refs[h][pl.ds(j, 1), :]
            m = m_refs[h][...]
            m_new = jnp.maximum(m, jnp.max(s, axis=0, keepdims=True) + bias)
            a = jnp.exp2(m - m_new)
            p = jnp.exp2((s + (bias - m_new)).astype(BF16))
            m_refs[h][...] = m_new
            acc_refs[h][...] = a * acc_refs[h][...] + jnp.dot(vt_ref[h, j], p, preferred_element_type=F32)

    buf_b, buf_a = s_bufs
    issue_scores(qi, buf_b)
    for h in range(hps):
        qm = jnp.where(lane_head == h, q, 0.0)
        gate = lax.dot_general(km, qm, nt, precision=HIGHEST, preferred_element_type=F32)
        bias_refs[h][...] = _select_blocks(gate, qi, axis=0)
        m_refs[h][...] = jnp.full(m_refs[h].shape, NEG_INF, F32)
        acc_refs[h][...] = jnp.zeros(acc_refs[h].shape, F32)
    issue_scores(0, buf_a)
    consume(qi, buf_b, own=True)

    def quad(i, carry):
        j = 4 * i
        issue_scores(j + 1, buf_b)
        consume(j, buf_a)
        issue_scores(j + 2, buf_a)
        consume(j + 1, buf_b)
        issue_scores(j + 3, buf_b)
        consume(j + 2, buf_a)
        issue_scores(j + 4, buf_a)
        consume(j + 3, buf_b)
        return carry

    lax.fori_loop(0, qi // 4, quad, 0)
    j0 = (qi // 4) * 4
    rem = qi - j0

    @pl.when(rem >= 1)
    def _():
        issue_scores(j0 + 1, buf_b)
        consume(j0, buf_a)

    @pl.when(rem >= 2)
    def _():
        issue_scores(j0 + 2, buf_a)
        consume(j0 + 1, buf_b)

    @pl.when(rem == 3)
    def _():
        consume(j0 + 2, buf_a)

    out_t = jnp.concatenate(
        [acc_refs[h][:HEAD_DIM, :] / acc_refs[h][HEAD_DIM:HEAD_DIM + 1, :] for h in range(hps)], axis=0)
    o_ref[...] = out_t.T


def _moba_prompt(q, kb, vt, km, batch, seq_len):
    t = q.shape[0]
    nb = seq_len // MOBA_BLOCK
    tq = MOBA_BLOCK
    hps = HEADS_PER_STEP
    gw = hps * HEAD_DIM
    scratch = ([pltpu.VMEM((tq, gw), BF16) for _ in range(hps)]
               + [pltpu.VMEM((nb, tq), F32) for _ in range(hps)]
               + [pltpu.VMEM((1, tq), F32) for _ in range(hps)]
               + [pltpu.VMEM((VT_ROWS, tq), F32) for _ in range(hps)]
               + [pltpu.VMEM((MOBA_BLOCK, tq), F32) for _ in range(2 * hps)])
    return pl.pallas_call(
        _moba_prompt_kernel,
        grid=(batch, N_HEADS // hps, nb),
        in_specs=[
            pl.BlockSpec((tq, gw), lambda b, g, i: (b * nb + i, g)),
            pl.BlockSpec((seq_len, gw), lambda b, g, i: (b, g)),
            pl.BlockSpec((hps, nb, VT_ROWS, MOBA_BLOCK), lambda b, g, i: (g, b, 0, 0)),
            pl.BlockSpec((1, nb, gw), lambda b, g, i: (b, 0, g)),
        ],
        out_specs=pl.BlockSpec((tq, gw), lambda b, g, i: (b * nb + i, g)),
        out_shape=jax.ShapeDtypeStruct((t, ATTN_WIDTH), F32),
        scratch_shapes=scratch,
        compiler_params=_params("parallel", "parallel", "arbitrary"),
        name="moba_prompt",
    )(q, kb, vt, km)


SEQS_PER_STEP = 2


def _moba_sample_kernel(n_pages, pt_ref, q_ref, kn_ref, vn_ref, *rest):
    del pt_ref
    o_ref = rest[-1]
    for s in range(q_ref.shape[0]):
        pages = rest[2 * n_pages * s:2 * n_pages * (s + 1)]
        o_ref[s] = _moba_sample_one(q_ref[s], kn_ref[s], vn_ref[s], pages[:n_pages], pages[n_pages:])


def _moba_sample_one(q, k_new, v_new, k_pages, v_pages):
    n_pages = len(k_pages)
    nq = q.shape[0]
    rows = N_HEADS * nq
    ppb = MOBA_BLOCK // PAGE_SIZE
    nb_past = n_pages // ppb
    nt = (((1,), (1,)), ((), ()))

    q_rep = jnp.concatenate([q] * N_HEADS, axis=0)
    r_i = lax.broadcasted_iota(jnp.int32, (rows, ATTN_WIDTH), 0)
    c_i = lax.broadcasted_iota(jnp.int32, (rows, ATTN_WIDTH), 1)
    head_mask = (c_i // HEAD_DIM) == (r_i // nq)
    qmat = jnp.where(head_mask, q_rep, 0.0)
    qmat_b = qmat.astype(BF16)

    def page_t(ref):
        return ref[0].reshape(ATTN_WIDTH, PAGE_SIZE)

    blk_lane = lax.broadcasted_iota(jnp.int32, (ATTN_WIDTH, nb_past), 1)
    kmean_t = jnp.zeros((ATTN_WIDTH, nb_past), F32)
    for n in range(nb_past):
        tot = page_t(k_pages[n * ppb])
        for pg in range(n * ppb + 1, (n + 1) * ppb):
            tot = tot + page_t(k_pages[pg])
        col = jnp.sum(tot, axis=1, keepdims=True) * (1.0 / MOBA_BLOCK)
        kmean_t = jnp.where(blk_lane == n, col, kmean_t)
    gate = jnp.dot(qmat, kmean_t, precision=HIGHEST, preferred_element_type=F32)
    bias = _select_blocks(gate, nb_past, axis=1)

    s_parts = []
    for pg in range(n_pages):
        sp = jnp.dot(qmat_b, page_t(k_pages[pg]).astype(BF16), preferred_element_type=F32)
        n = pg // ppb
        s_parts.append(sp + bias[:, n:n + 1])
    sn = lax.dot_general(qmat_b, k_new.astype(BF16), nt, preferred_element_type=F32)
    key_i = lax.broadcasted_iota(jnp.int32, (rows, nq), 1)
    qry_i = lax.broadcasted_iota(jnp.int32, (rows, nq), 0) % nq
    sn = jnp.where(key_i <= qry_i, sn, NEG_INF)

    m_el = s_parts[0]
    for sp in s_parts[1:]:
        m_el = jnp.maximum(m_el, sp)
    m = jnp.maximum(jnp.max(m_el, axis=1, keepdims=True), jnp.max(sn, axis=1, keepdims=True))
    pn = jnp.exp(sn - m)
    acc = jnp.dot(pn.astype(BF16), v_new.astype(BF16), preferred_element_type=F32)
    l_el = jnp.zeros((rows, PAGE_SIZE), F32)
    for pg in range(n_pages):
        p = jnp.exp(s_parts[pg] - m)
        l_el = l_el + p
        acc = acc + lax.dot_general(p.astype(BF16), page_t(v_pages[pg]).astype(BF16), nt,
                                    preferred_element_type=F32)
    l = jnp.sum(pn, axis=1, keepdims=True) + jnp.sum(l_el, axis=1, keepdims=True)
    out = jnp.where(head_mask, acc / l, 0.0)
    res = out[0:nq]
    for h in range(1, N_HEADS):
        res = res + out[h * nq:(h + 1) * nq]
    return res


def _moba_sample(q, k_new, v_new, cache_k, cache_v, page_table, layer, n_pool):
    bsz, nq, _ = q.shape
    n_pages = page_table.shape[1]
    base = layer * n_pool
    spp = SEQS_PER_STEP if bsz % SEQS_PER_STEP == 0 else 1
    tok = pl.BlockSpec((spp, nq, ATTN_WIDTH), lambda b, pt: (b, 0, 0))

    def page_spec(s, pg):
        return pl.BlockSpec((1, N_HEADS, HEAD_DIM, PAGE_SIZE),
                            lambda b, pt: (base + pt[b * spp + s, pg], 0, 0, 0))

    page_specs, page_args = [], []
    for s in range(spp):
        page_specs += [page_spec(s, pg) for pg in range(n_pages)] * 2
        page_args += [cache_k] * n_pages + [cache_v] * n_pages
    grid_spec = pltpu.PrefetchScalarGridSpec(
        num_scalar_prefetch=1,
        grid=(bsz // spp,),
        in_specs=[tok, tok, tok] + page_specs,
        out_specs=tok,
    )
    return pl.pallas_call(
        functools.partial(_moba_sample_kernel, n_pages),
        grid_spec=grid_spec,
        out_shape=jax.ShapeDtypeStruct((bsz, nq, ATTN_WIDTH), F32),
        compiler_params=_params("parallel"),
        name="moba_sample",
    )(page_table, q, k_new, v_new, *page_args)


def _s5_discretise(a_re, a_im, log_dt, b_re, b_im):
    dt = jnp.exp(log_dt.astype(F32))[:, None]
    ar, ai = a_re.astype(F32), a_im.astype(F32)
    mag = jnp.exp(ar * dt)
    lr, li = mag * jnp.cos(ai * dt), mag * jnp.sin(ai * dt)
    den = ar * ar + ai * ai
    nr = lr - 1.0
    fr = (nr * ar + li * ai) / den
    fi = (li * ar - nr * ai) / den
    br_, bi_ = b_re.astype(F32), b_im.astype(F32)
    bbr = fr[..., None] * br_ - fi[..., None] * bi_
    bbi = fr[..., None] * bi_ + fi[..., None] * br_
    return ar * dt, ai * dt, lr, li, bbr, bbi


def _s5_prompt_mats(a_re, a_im, log_dt, b_re, b_im, c_re, c_im):
    c = SSM_CHUNK
    g_, p_, cdim = SSM_GROUPS, SSM_STATE, SSM_GROUP_DIM
    adr, adi, _, _, bbr, bbi = _s5_discretise(a_re, a_im, log_dt, b_re, b_im)
    tau = jnp.arange(c + 1, dtype=F32)[:, None, None]
    pmag = jnp.exp(adr[None] * tau)
    pw_r, pw_i = pmag * jnp.cos(adi[None] * tau), pmag * jnp.sin(adi[None] * tau)
    cr, ci = c_re.astype(F32), c_im.astype(F32)
    lb_r = pw_r[..., None] * bbr[None] - pw_i[..., None] * bbi[None]
    lb_i = pw_r[..., None] * bbi[None] + pw_i[..., None] * bbr[None]
    kern = (jnp.einsum('gop,tgpi->tgoi', cr, lb_r[:c], precision=HIGHEST)
            - jnp.einsum('gop,tgpi->tgoi', ci, lb_i[:c], precision=HIGHEST))
    s_i = jnp.arange(c)[:, None]
    t_i = jnp.arange(c)[None, :]
    lag = t_i - s_i
    toep = jnp.where((lag >= 0)[:, :, None, None, None], kern[jnp.clip(lag, 0, c - 1)], 0.0)
    toep = toep.transpose(2, 0, 4, 1, 3).reshape(g_, c * cdim, c * cdim)
    w_r = lb_r[:c][::-1].transpose(1, 0, 3, 2).reshape(g_, c * cdim, p_)
    w_i = lb_i[:c][::-1].transpose(1, 0, 3, 2).reshape(g_, c * cdim, p_)
    pr, pi = pw_r[1:], pw_i[1:]
    v_r = cr[None] * pr[:, :, None, :] - ci[None] * pi[:, :, None, :]
    v_i = -cr[None] * pi[:, :, None, :] - ci[None] * pr[:, :, None, :]
    v_r = v_r.transpose(1, 3, 0, 2).reshape(g_, p_, c * cdim)
    v_i = v_i.transpose(1, 3, 0, 2).reshape(g_, p_, c * cdim)
    decay = jnp.stack([pw_r[c], pw_i[c]], axis=1)
    return toep.astype(BF16), w_r.astype(BF16), w_i.astype(BF16), v_r.astype(BF16), v_i.astype(BF16), decay


def _s5_sample_mats(a_re, a_im, log_dt, b_re, b_im, c_re, c_im):
    g_, p_, cdim = SSM_GROUPS, SSM_STATE, SSM_GROUP_DIM
    _, _, lr, li, bbr, bbi = _s5_discretise(a_re, a_im, log_dt, b_re, b_im)
    eye = jnp.eye(g_, dtype=F32)
    b_bd_r = (eye[:, None, :, None] * bbr.transpose(0, 2, 1)[:, :, None, :]).reshape(g_ * cdim, g_ * p_)
    b_bd_i = (eye[:, None, :, None] * bbi.transpose(0, 2, 1)[:, :, None, :]).reshape(g_ * cdim, g_ * p_)
    cr, ci = c_re.astype(F32), c_im.astype(F32)
    c_bd_r = (eye[:, None, :, None] * cr.transpose(0, 2, 1)[:, :, None, :]).reshape(g_ * p_, g_ * cdim)
    c_bd_i = (eye[:, None, :, None] * ci.transpose(0, 2, 1)[:, :, None, :]).reshape(g_ * p_, g_ * cdim)
    return lr.reshape(1, g_ * p_), li.reshape(1, g_ * p_), b_bd_r, b_bd_i, c_bd_r, c_bd_i


def _s5_prompt_kernel(n_batch, u_ref, toep_ref, wr_ref, wi_ref, vr_ref, vi_ref, dec_ref,
                      y_ref, hr_ref, hi_ref, sr_s, si_s, pr_s, pi_s):
    u = u_ref[0]
    n_chunks = u.shape[0]
    cpb = n_chunks // n_batch
    sr_s[...] = jnp.dot(u, wr_ref[0], preferred_element_type=F32)
    si_s[...] = jnp.dot(u, wi_ref[0], preferred_element_type=F32)
    dr = dec_ref[0, 0:1, :]
    di = dec_ref[0, 1:2, :]

    def step(kk, carry):
        new = []
        for b in range(n_batch):
            hr, hi = carry[2 * b], carry[2 * b + 1]
            r = b * cpb + kk
            pr_s[pl.ds(r, 1), :] = hr
            pi_s[pl.ds(r, 1), :] = hi
            new.append(dr * hr - di * hi + sr_s[pl.ds(r, 1), :])
            new.append(dr * hi + di * hr + si_s[pl.ds(r, 1), :])
        return tuple(new)

    zero = jnp.zeros((1, SSM_STATE), F32)
    fin = lax.fori_loop(0, cpb, step, (zero,) * (2 * n_batch))
    for b in range(n_batch):
        hr_ref[0, b:b + 1, :] = fin[2 * b]
        hi_ref[0, b:b + 1, :] = fin[2 * b + 1]
    y = jnp.dot(u, toep_ref[0], preferred_element_type=F32)
    y = y + jnp.dot(pr_s[...].astype(BF16), vr_ref[0], preferred_element_type=F32)
    y = y + jnp.dot(pi_s[...].astype(BF16), vi_ref[0], preferred_element_type=F32)
    y_ref[0] = y


def _s5_prompt(u_g, mats, n_batch):
    toep, w_r, w_i, v_r, v_i, decay = mats
    g_, n_chunks, cw = u_g.shape
    per_g = lambda *shape: pl.BlockSpec((1,) + shape, lambda g: (g,) + (0,) * len(shape))
    st = jax.ShapeDtypeStruct((g_, n_batch, SSM_STATE), F32)
    return pl.pallas_call(
        functools.partial(_s5_prompt_kernel, n_batch),
        grid=(g_,),
        in_specs=[per_g(n_chunks, cw), per_g(cw, cw), per_g(cw, SSM_STATE), per_g(cw, SSM_STATE),
                  per_g(SSM_STATE, cw), per_g(SSM_STATE, cw), per_g(2, SSM_STATE)],
        out_specs=(per_g(n_chunks, cw), per_g(n_batch, SSM_STATE), per_g(n_batch, SSM_STATE)),
        out_shape=(jax.ShapeDtypeStruct((g_, n_chunks, cw), F32), st, st),
        scratch_shapes=[pltpu.VMEM((n_chunks, SSM_STATE), F32) for _ in range(4)],
        compiler_params=_params("parallel"),
        name="s5_prompt",
    )(u_g, toep, w_r, w_i, v_r, v_i, decay)


def _s5_sample_kernel(u_ref, h0r_ref, h0i_ref, lr_ref, li_ref, br_ref, bi_ref, cr_ref, ci_ref,
                      y_ref, hr_ref, hi_ref):
    n_steps = u_ref.shape[0]
    lr, li = lr_ref[...], li_ref[...]
    hr, hi = h0r_ref[...], h0i_ref[...]
    for t in range(n_steps):
        u = u_ref[t]
        bur = jnp.dot(u, br_ref[...], precision=HIGHEST, preferred_element_type=F32)
        bui = jnp.dot(u, bi_ref[...], precision=HIGHEST, preferred_element_type=F32)
        hr, hi = lr * hr - li * hi + bur, lr * hi + li * hr + bui
        y_ref[t] = (jnp.dot(hr, cr_ref[...], precision=HIGHEST, preferred_element_type=F32)
                    - jnp.dot(hi, ci_ref[...], precision=HIGHEST, preferred_element_type=F32))
    hr_ref[...] = hr
    hi_ref[...] = hi


def _s5_sample(u_t, h0r, h0i, mats):
    n_steps, bsz, _ = u_t.shape
    st = jax.ShapeDtypeStruct(h0r.shape, F32)
    return pl.pallas_call(
        _s5_sample_kernel,
        out_shape=(jax.ShapeDtypeStruct((n_steps, bsz, SSM_WIDTH), F32), st, st),
        compiler_params=pltpu.CompilerParams(vmem_limit_bytes=VMEM_LIMIT),
        name="s5_sample",
    )(u_t, h0r, h0i, *mats)


def _layer_norm(x, g, b):
    mu = jnp.mean(x, axis=-1, keepdims=True)
    xc = x - mu
    var = jnp.mean(xc * xc, axis=-1, keepdims=True)
    return xc * lax.rsqrt(var + LN_EPS) * g + b


def _rms_gain(y, g):
    return y * lax.rsqrt(jnp.mean(y * y, axis=-1, keepdims=True) + RMS_EPS) * g


def _sigmoid(x):
    return 1.0 / (1.0 + jnp.exp(-x))


def _gelu_tanh(x):
    return 0.5 * x * (1.0 + jnp.tanh(math.sqrt(2.0 / math.pi) * (x + 0.044715 * (x * x * x))))


def _route(lg, le):
    tm = lg.shape[0]
    gi = lax.broadcasted_iota(jnp.int32, lg.shape, 1)
    gmax = jnp.max(lg, axis=-1, keepdims=True)
    g_sel = jnp.min(jnp.where(lg == gmax, gi, N_GROUPS), axis=-1, keepdims=True)
    p_sel = 1.0 / jnp.sum(jnp.exp(lg - gmax), axis=-1, keepdims=True)
    ei = lax.broadcasted_iota(jnp.int32, (tm, N_EXPERTS), 1)
    cand = jnp.where((ei // EXPERTS_PER_GROUP) == g_sel, le, -jnp.inf)
    v0 = jnp.max(cand, axis=-1, keepdims=True)
    i0 = jnp.min(jnp.where(cand == v0, ei, N_EXPERTS), axis=-1, keepdims=True)
    cand = jnp.where(ei == i0, -jnp.inf, cand)
    v1 = jnp.max(cand, axis=-1, keepdims=True)
    i1 = jnp.min(jnp.where(cand == v1, ei, N_EXPERTS), axis=-1, keepdims=True)
    e1 = jnp.exp(v1 - v0)
    w0 = p_sel / (1.0 + e1)
    w1 = p_sel * e1 / (1.0 + e1)
    return jnp.where(ei == i0, w0, 0.0) + jnp.where(ei == i1, w1, 0.0)


def _post_mixer_kernel(y_ref, u_ref, a_ref, x_ref, d_ref, wglu_ref, bglu_ref, ga_ref, gs_ref,
                       wout_ref, g1_ref, b1_ref, wr_ref, br_ref, x1_ref, comb_ref):
    tm = x_ref.shape[0]
    sub = min(tm, POST_MIXER_SUB_ROWS)
    for r in range(tm // sub):
        rows = pl.ds(r * sub, sub)
        y = y_ref[rows, :] + d_ref[...] * u_ref[rows, :]
        g = _gelu_tanh(y)
        z = g * _sigmoid(jnp.dot(g.astype(BF16), wglu_ref[...], preferred_element_type=F32) + bglu_ref[...])
        mix_in = jnp.concatenate([_rms_gain(a_ref[rows, :], ga_ref[...]), _rms_gain(z, gs_ref[...])], axis=-1)
        mixed = jnp.dot(mix_in.astype(BF16), wout_ref[...], preferred_element_type=F32)
        x1 = _layer_norm(ALPHA * x_ref[rows, :] + mixed, g1_ref[...], b1_ref[...])
        x1_ref[rows, :] = x1
        logits = jnp.dot(x1, wr_ref[...], precision=HIGHEST, preferred_element_type=F32) + br_ref[...]
        comb_ref[rows, :] = _route(logits[:, :N_GROUPS], logits[:, N_GROUPS:N_GROUPS + N_EXPERTS])


def _post_mixer(y, u, attn, x, lp, tm):
    t = x.shape[0]
    tm = min(tm, t)
    row = lambda i: (i, 0)
    full = lambda a: pl.BlockSpec(a.shape, lambda i: (0, 0))
    weights = [lp['d'], lp['w_glu'], lp['b_glu'], lp['g_attn'], lp['g_ssm'], lp['w_out'],
               lp['ln1_g'], lp['ln1_b'], lp['w_router'], lp['b_router']]
    return pl.pallas_call(
        _post_mixer_kernel,
        grid=(t // tm,),
        in_specs=[pl.BlockSpec((tm, SSM_WIDTH), row), pl.BlockSpec((tm, SSM_WIDTH), row),
                  pl.BlockSpec((tm, ATTN_WIDTH), row), pl.BlockSpec((tm, D_MODEL), row)]
                 + [full(w) for w in weights],
        out_specs=(pl.BlockSpec((tm, D_MODEL), row), pl.BlockSpec((tm, N_EXPERTS), row)),
        out_shape=(jax.ShapeDtypeStruct((t, D_MODEL), F32), jax.ShapeDtypeStruct((t, N_EXPERTS), F32)),
        compiler_params=_params("parallel"),
        name="post_mixer",
    )(y, u, attn, x, *weights)


def _moe_kernel(x_ref, comb_ref, wg_ref, wu_ref, wd_ref, g2_ref, b2_ref, o_ref, acc_ref):
    e = pl.program_id(1)

    @pl.when(e == 0)
    def _():
        acc_ref[...] = jnp.zeros_like(acc_ref)

    xb = x_ref[...].astype(BF16)
    hg = jnp.dot(xb, wg_ref[0].astype(BF16), preferred_element_type=F32)
    hu = jnp.dot(xb, wu_ref[0].astype(BF16), preferred_element_type=F32)
    h = hg * _sigmoid(hg) * hu
    y = jnp.dot(h.astype(BF16), wd_ref[0].astype(BF16), preferred_element_type=F32)
    comb = comb_ref[...]
    ei = lax.broadcasted_iota(jnp.int32, comb.shape, 1)
    w = jnp.sum(jnp.where(ei == e, comb, 0.0), axis=-1, keepdims=True)
    acc_ref[...] += w * y

    @pl.when(e == pl.num_programs(1) - 1)
    def _():
        o_ref[...] = _layer_norm(ALPHA * x_ref[...] + acc_ref[...], g2_ref[...], b2_ref[...])


def _moe(x1, comb, lp, tm):
    t = x1.shape[0]
    layer = lp['layer']
    row = lambda i, e: (i, 0)
    return pl.pallas_call(
        _moe_kernel,
        grid=(t // tm, N_EXPERTS),
        in_specs=[
            pl.BlockSpec((tm, D_MODEL), row),
            pl.BlockSpec((tm, N_EXPERTS), row),
            pl.BlockSpec((None, 1, D_MODEL, D_EXPERT), lambda i, e: (layer, e, 0, 0)),
            pl.BlockSpec((None, 1, D_MODEL, D_EXPERT), lambda i, e: (layer, e, 0, 0)),
            pl.BlockSpec((None, 1, D_EXPERT, D_MODEL), lambda i, e: (layer, e, 0, 0)),
            pl.BlockSpec((1, D_MODEL), lambda i, e: (0, 0)),
            pl.BlockSpec((1, D_MODEL), lambda i, e: (0, 0)),
        ],
        out_specs=pl.BlockSpec((tm, D_MODEL), row),
        out_shape=jax.ShapeDtypeStruct((t, D_MODEL), F32),
        scratch_shapes=[pltpu.VMEM((tm, D_MODEL), F32)],
        compiler_params=_params("parallel", "arbitrary"),
        name="moe",
    )(x1, comb, lp['w_gate'], lp['w_up'], lp['w_down'], lp['ln2_g'], lp['ln2_b'])


def _prompt_layer(x, lp, batch, seq_len, cos_t, sin_t, layer, depth, kbuf, vbuf):
    t = x.shape[0]
    nb = seq_len // MOBA_BLOCK
    kbuf, vbuf, u, q, kb, vt, km = _inproj_prompt(x, lp['w_in'], cos_t, sin_t, seq_len,
                                                   layer, depth, kbuf, vbuf)
    attn = _moba_prompt(q, kb, vt, km.reshape(batch, nb, ATTN_WIDTH), batch, seq_len)
    c = SSM_CHUNK
    u_g = (u.astype(BF16).reshape(t // c, c, SSM_GROUPS, SSM_GROUP_DIM)
           .transpose(2, 0, 1, 3).reshape(SSM_GROUPS, t // c, c * SSM_GROUP_DIM))
    y_g, hr, hi = _s5_prompt(u_g, lp['s5_prompt'], batch)
    y = (y_g.reshape(SSM_GROUPS, t // c, c, SSM_GROUP_DIM)
         .transpose(1, 2, 0, 3).reshape(t, SSM_WIDTH))
    x1, comb = _post_mixer(y, u, attn, x, lp, tm=512)
    x2 = _moe(x1, comb, lp, tm=1024)
    return x2, kbuf, vbuf, hr.transpose(1, 0, 2), hi.transpose(1, 0, 2)


def _sample_layer(x, lp, bsz, nq, cos_t, sin_t, cache_k, cache_v, page_table, layer, n_pool,
                  h0r, h0i):
    t = x.shape[0]
    q, k, v, u = _inproj_sample(x, lp['w_in'], cos_t, sin_t)
    to3 = lambda a: a.reshape(bsz, nq, ATTN_WIDTH)
    attn = _moba_sample(to3(q), to3(k), to3(v), cache_k, cache_v, page_table, layer, n_pool)
    attn = attn.reshape(t, ATTN_WIDTH)
    u_t = u.reshape(bsz, nq, SSM_WIDTH).transpose(1, 0, 2)
    y_t, hr, hi = _s5_sample(u_t, h0r, h0i, lp['s5_sample'])
    y = y_t.transpose(1, 0, 2).reshape(t, SSM_WIDTH)
    x1, comb = _post_mixer(y, u, attn, x, lp, tm=256)
    x2 = _moe(x1, comb, lp, tm=t)
    return x2, k, v, hr, hi


def _all_layer_params(w_in, w_out, norm_attn_g, norm_ssm_g, ssm_a_re, ssm_a_im, ssm_log_dt,
                      ssm_b_re, ssm_b_im, ssm_c_re, ssm_c_im, ssm_d, w_glu, b_glu, ln1_g, ln1_b,
                      w_group, b_group, w_expert, b_expert, w_gate, w_up, w_down, ln2_g, ln2_b):
    depth = w_in.shape[0]
    rowv = lambda a: a.reshape(depth, 1, -1).astype(F32)
    s5_args = (ssm_a_re, ssm_a_im, ssm_log_dt, ssm_b_re, ssm_b_im, ssm_c_re, ssm_c_im)
    pad = LANES - N_GROUPS - N_EXPERTS
    w_router = jnp.concatenate([w_group, w_expert, jnp.zeros((depth, D_MODEL, pad), F32)], axis=2)
    b_router = jnp.concatenate([b_group, b_expert, jnp.zeros((depth, pad), F32)], axis=1)
    return dict(
        w_in=w_in.astype(BF16), w_out=w_out.astype(BF16), w_glu=w_glu.astype(BF16),
        g_attn=rowv(norm_attn_g), g_ssm=rowv(norm_ssm_g), d=rowv(ssm_d), b_glu=rowv(b_glu),
        ln1_g=rowv(ln1_g), ln1_b=rowv(ln1_b), ln2_g=rowv(ln2_g), ln2_b=rowv(ln2_b),
        w_router=w_router.astype(F32), b_router=rowv(b_router),
        s5_prompt=jax.vmap(_s5_prompt_mats)(*s5_args), s5_sample=jax.vmap(_s5_sample_mats)(*s5_args),
    ), dict(w_gate=w_gate, w_up=w_up, w_down=w_down)


def _layer_params(stacked, experts, l):
    lp = jax.tree.map(lambda a: a[l], stacked)
    return dict(lp, layer=l, **experts)


def kernel(x_prompt, x_sample, cache_k, cache_v, state_ssm_re, state_ssm_im, page_table, w_in, w_out, norm_attn_g, norm_ssm_g, ssm_a_re, ssm_a_im, ssm_log_dt, ssm_b_re, ssm_b_im, ssm_c_re, ssm_c_im, ssm_d, w_glu, b_glu, ln1_g, ln1_b, w_group, b_group, w_expert, b_expert, w_gate, w_up, w_down, ln2_g, ln2_b):
    batch, seq_len, _ = x_prompt.shape
    dec_batch, dec_seq, _ = x_sample.shape
    depth, n_pool = cache_k.shape[0], cache_k.shape[1]
    n_pages = page_table.shape[1]
    past_len = n_pages * PAGE_SIZE
    assert seq_len % 512 == 0 and past_len % MOBA_BLOCK == 0 and dec_seq <= MOBA_BLOCK

    cos_p, sin_p = _rope_tables(jnp.arange(seq_len, dtype=jnp.int32))
    pos_s = past_len + jnp.arange(dec_seq, dtype=jnp.int32)
    cos_s, sin_s = _rope_tables(jnp.tile(pos_s, dec_batch))

    to_pages = lambda c: c.transpose(0, 1, 3, 4, 2).reshape(depth * n_pool, N_HEADS, HEAD_DIM, PAGE_SIZE)
    ck, cv = to_pages(cache_k), to_pages(cache_v)
    xp = x_prompt.reshape(batch * seq_len, D_MODEL)
    xs = x_sample.reshape(dec_batch * dec_seq, D_MODEL)
    kbuf = jnp.zeros((depth, batch, N_HEADS, HEAD_DIM, seq_len), F32)
    vbuf = jnp.zeros((depth, batch, N_HEADS, HEAD_DIM, seq_len), F32)
    outs = [[] for _ in range(6)]
    stacked, experts = _all_layer_params(
        w_in, w_out, norm_attn_g, norm_ssm_g, ssm_a_re, ssm_a_im, ssm_log_dt, ssm_b_re, ssm_b_im,
        ssm_c_re, ssm_c_im, ssm_d, w_glu, b_glu, ln1_g, ln1_b, w_group, b_group, w_expert, b_expert,
        w_gate, w_up, w_down, ln2_g, ln2_b)
    for l in range(depth):
        lp = _layer_params(stacked, experts, l)
        xp, kbuf, vbuf, hrp, hip = _prompt_layer(xp, lp, batch, seq_len, cos_p, sin_p, l, depth, kbuf, vbuf)
        h0r = state_ssm_re[l].reshape(dec_batch, SSM_GROUPS * SSM_STATE)
        h0i = state_ssm_im[l].reshape(dec_batch, SSM_GROUPS * SSM_STATE)
        xs, kn, vn, hrs, his = _sample_layer(xs, lp, dec_batch, dec_seq, cos_s, sin_s, ck, cv,
                                             page_table, l, n_pool, h0r, h0i)
        outs[0].append(hrp)
        outs[1].append(hip)
        outs[2].append(kn.reshape(dec_batch, dec_seq, N_HEADS, HEAD_DIM))
        outs[3].append(vn.reshape(dec_batch, dec_seq, N_HEADS, HEAD_DIM))
        outs[4].append(hrs.reshape(dec_batch, SSM_GROUPS, SSM_STATE))
        outs[5].append(his.reshape(dec_batch, SSM_GROUPS, SSM_STATE))
    return (xp.reshape(batch, seq_len, D_MODEL), xs.reshape(dec_batch, dec_seq, D_MODEL),
            kbuf.transpose(0, 1, 4, 2, 3), vbuf.transpose(0, 1, 4, 2, 3), *[jnp.stack(o) for o in outs])
```

```python
import functools
import math

import jax
import jax.numpy as jnp
from jax import lax
from jax.experimental import pallas as pl
from jax.experimental.pallas import tpu as pltpu

F32 = jnp.float32
BF16 = jnp.bfloat16
HIGHEST = lax.Precision.HIGHEST

D_MODEL = 1024
ATTN_WIDTH = 512
SSM_WIDTH = 512
HEAD_DIM = 64
N_HEADS = 8
MOBA_BLOCK = 256
MOBA_TOPK = 3
PAGE_SIZE = 128
ROPE_THETA = 10000.0
SSM_GROUP_DIM = 16
SSM_GROUPS = 32
SSM_STATE = 64
N_GROUPS = 4
EXPERTS_PER_GROUP = 4
N_EXPERTS = 16
D_EXPERT = 512
DEPTH = 4
ALPHA = (2.0 * DEPTH) ** 0.25
LN_EPS = 1e-5
RMS_EPS = 1e-6
NEG_INF = -1e30
LOG2E = 1.4426950408889634
PROJ_WIDTH = 3 * ATTN_WIDTH + SSM_WIDTH

LANES = 128
SSM_CHUNK = 32
VT_ROWS = HEAD_DIM + 16
POST_MIXER_SUB_ROWS = 256
VMEM_LIMIT = 48 * 1024 * 1024


def _params(*sem):
    return pltpu.CompilerParams(dimension_semantics=sem, vmem_limit_bytes=VMEM_LIMIT)


def _rope_tables(pos):
    inv_freq = 1.0 / jnp.power(ROPE_THETA, jnp.arange(0, HEAD_DIM, 2, dtype=F32) / HEAD_DIM)
    ang = pos.astype(F32)[:, None] * inv_freq[None, :]
    cos, sin = jnp.cos(ang), jnp.sin(ang)
    cos_t = jnp.concatenate([cos, cos, cos, cos], axis=-1)
    sin_t = jnp.concatenate([-sin, sin, -sin, sin], axis=-1)
    return cos_t, sin_t


def _rope(x, cos, sin):
    tm = x.shape[0]
    lane = lax.broadcasted_iota(jnp.int32, (tm, LANES), 1)
    first_half = (lane % HEAD_DIM) < (HEAD_DIM // 2)
    outs = []
    for c in range(x.shape[1] // LANES):
        xc = x[:, c * LANES:(c + 1) * LANES]
        partner = jnp.where(first_half, pltpu.roll(xc, LANES - HEAD_DIM // 2, 1),
                            pltpu.roll(xc, HEAD_DIM // 2, 1))
        outs.append(xc * cos + partner * sin)
    return jnp.concatenate(outs, axis=1)


def _inproj_prompt_kernel(x_ref, w_ref, cos_ref, sin_ref, *rest):
    k_ref, v_ref, u_ref, q_ref, kb_ref, vt_ref, km_ref = rest[-7:]
    tm = x_ref.shape[0]
    proj = jnp.dot(x_ref[...].astype(BF16), w_ref[...], preferred_element_type=F32)
    cos, sin = cos_ref[...], sin_ref[...]
    q = _rope(proj[:, :ATTN_WIDTH], cos, sin)
    k = _rope(proj[:, ATTN_WIDTH:2 * ATTN_WIDTH], cos, sin)
    v = proj[:, 2 * ATTN_WIDTH:3 * ATTN_WIDTH]
    u_ref[...] = proj[:, 3 * ATTN_WIDTH:]
    q_ref[...] = q * (HEAD_DIM ** -0.5 * LOG2E)
    kb_ref[...] = k.astype(BF16)
    vt = v.T
    k_ref[0, 0] = k.T.reshape(N_HEADS, HEAD_DIM, tm)
    v_ref[0, 0] = vt.reshape(N_HEADS, HEAD_DIM, tm)
    pad_row = lax.broadcasted_iota(jnp.int32, (N_HEADS, VT_ROWS - HEAD_DIM, MOBA_BLOCK), 1)
    ones_pad = jnp.where(pad_row == 0, 1.0, 0.0).astype(BF16)
    for blk in range(tm // MOBA_BLOCK):
        sl = slice(blk * MOBA_BLOCK, (blk + 1) * MOBA_BLOCK)
        vt_blk = vt[:, sl].reshape(N_HEADS, HEAD_DIM, MOBA_BLOCK).astype(BF16)
        vt_ref[:, blk] = jnp.concatenate([vt_blk, ones_pad], axis=1)
        km_ref[blk] = jnp.mean(k[sl], axis=0, keepdims=True)


def _inproj_prompt(x, w_in, cos_t, sin_t, seq_len, layer, depth, kbuf, vbuf, tm=512):
    t = x.shape[0]
    n_tiles_seq = seq_len // tm
    nblk = t // MOBA_BLOCK
    bpt = tm // MOBA_BLOCK
    row = lambda i: (i, 0)
    kv_sds = jax.ShapeDtypeStruct((depth, t // seq_len, N_HEADS, HEAD_DIM, seq_len), F32)
    out_shape = (
        kv_sds,
        kv_sds,
        jax.ShapeDtypeStruct((t, SSM_WIDTH), F32),
        jax.ShapeDtypeStruct((t, ATTN_WIDTH), F32),
        jax.ShapeDtypeStruct((t, ATTN_WIDTH), BF16),
        jax.ShapeDtypeStruct((N_HEADS, nblk, VT_ROWS, MOBA_BLOCK), BF16),
        jax.ShapeDtypeStruct((nblk, 1, ATTN_WIDTH), F32),
    )
    kv_spec = pl.BlockSpec((1, 1, N_HEADS, HEAD_DIM, tm),
                           lambda i: (layer, i // n_tiles_seq, 0, 0, i % n_tiles_seq))
    any_spec = pl.BlockSpec(memory_space=pl.ANY)
    in_specs = [
        pl.BlockSpec((tm, D_MODEL), row),
        pl.BlockSpec((D_MODEL, PROJ_WIDTH), lambda i: (0, 0)),
        pl.BlockSpec((tm, LANES), lambda i: (i % n_tiles_seq, 0)),
        pl.BlockSpec((tm, LANES), lambda i: (i % n_tiles_seq, 0)),
    ]
    out_specs = (
        kv_spec, kv_spec,
        pl.BlockSpec((tm, SSM_WIDTH), row),
        pl.BlockSpec((tm, ATTN_WIDTH), row),
        pl.BlockSpec((tm, ATTN_WIDTH), row),
        pl.BlockSpec((N_HEADS, bpt, VT_ROWS, MOBA_BLOCK), lambda i: (0, i, 0, 0)),
        pl.BlockSpec((bpt, 1, ATTN_WIDTH), lambda i: (i, 0, 0)),
    )
    return pl.pallas_call(
        _inproj_prompt_kernel,
        grid=(t // tm,),
        in_specs=in_specs + [any_spec, any_spec],
        out_specs=out_specs,
        out_shape=out_shape,
        input_output_aliases={4: 0, 5: 1},
        compiler_params=_params("parallel"),
        name="inproj_prompt",
    )(x, w_in, cos_t, sin_t, kbuf, vbuf)


def _inproj_sample_kernel(x_ref, w_ref, cos_ref, sin_ref, q_ref, k_ref, v_ref, u_ref):
    proj = jnp.dot(x_ref[...].astype(BF16), w_ref[...], preferred_element_type=F32)
    cos, sin = cos_ref[...], sin_ref[...]
    q_ref[...] = _rope(proj[:, :ATTN_WIDTH], cos, sin) * (HEAD_DIM ** -0.5)
    k_ref[...] = _rope(proj[:, ATTN_WIDTH:2 * ATTN_WIDTH], cos, sin)
    v_ref[...] = proj[:, 2 * ATTN_WIDTH:3 * ATTN_WIDTH]
    u_ref[...] = proj[:, 3 * ATTN_WIDTH:]


def _inproj_sample(x, w_in, cos_t, sin_t, tm=256):
    t = x.shape[0]
    tm = min(tm, t)
    row = lambda i: (i, 0)
    sds = jax.ShapeDtypeStruct((t, ATTN_WIDTH), F32)
    return pl.pallas_call(
        _inproj_sample_kernel,
        grid=(t // tm,),
        in_specs=[
            pl.BlockSpec((tm, D_MODEL), row),
            pl.BlockSpec((D_MODEL, PROJ_WIDTH), lambda i: (0, 0)),
            pl.BlockSpec((tm, LANES), row),
            pl.BlockSpec((tm, LANES), row),
        ],
        out_specs=tuple(pl.BlockSpec((tm, ATTN_WIDTH), row) for _ in range(4)),
        out_shape=(sds, sds, sds, sds),
        compiler_params=_params("parallel"),
        name="inproj_sample",
    )(x, w_in, cos_t, sin_t)


def _select_blocks(gate, n_valid, axis):
    nb = gate.shape[axis]
    blk = lax.broadcasted_iota(jnp.int32, gate.shape, axis)
    gate = jnp.where(blk < n_valid, gate, NEG_INF)
    sel = jnp.zeros(gate.shape, dtype=jnp.bool_)
    for r in range(MOBA_TOPK):
        top = jnp.max(gate, axis=axis, keepdims=True)
        first = jnp.min(jnp.where(gate == top, blk, nb), axis=axis, keepdims=True)
        hit = blk == first
        sel = jnp.logical_or(sel, jnp.logical_and(hit, r < n_valid))
        gate = jnp.where(hit, -jnp.inf, gate)
    return jnp.where(sel, 0.0, NEG_INF).astype(F32)


HEADS_PER_STEP = 4


def _moba_prompt_kernel(q_ref, k_ref, vt_ref, km_ref, o_ref, *scratch):
    hps = HEADS_PER_STEP
    qm_refs, bias_refs = scratch[:hps], scratch[hps:2 * hps]
    m_refs, acc_refs = scratch[2 * hps:3 * hps], scratch[3 * hps:4 * hps]
    s_bufs = (scratch[4 * hps:5 * hps], scratch[5 * hps:6 * hps])
    qi = pl.program_id(2)
    q = q_ref[...]
    lane_head = lax.broadcasted_iota(jnp.int32, q.shape, 1) // HEAD_DIM
    km = km_ref[0]
    nt = (((1,), (1,)), ((), ()))
    for h in range(hps):
        qm_refs[h][...] = jnp.where(lane_head == h, q, 0.0).astype(BF16)

    def issue_scores(j, bufs):
        k_j = k_ref[pl.ds(pl.multiple_of(j * MOBA_BLOCK, MOBA_BLOCK), MOBA_BLOCK), :]
        for h in range(hps):
            bufs[h][...] = lax.dot_general(k_j, qm_refs[h][...], nt, preferred_element_type=F32)

    def consume(j, bufs, own=False):
        for h in range(hps):
            s = bufs[h][...]
            if own:
                key_i = lax.broadcasted_iota(jnp.int32, s.shape, 0)
                qry_i = lax.broadcasted_iota(jnp.int32, s.shape, 1)
                s = jnp.where(key_i <= qry_i, s, NEG_INF)
                bias = jnp.zeros((1, s.shape[1]), F32)
            else:
                bias = bias_refs[h][pl.ds(j, 1), :]
            m = m_refs[h][...]
            m_new = jnp.maximum(m, jnp.max(s, axis=0, keepdims=True) + bias)
            a = jnp.exp2(m - m_new)
            p = jnp.exp2((s + (bias - m_new)).astype(BF16))
            m_refs[h][...] = m_new
            acc_refs[h][...] = a * acc_refs[h][...] + jnp.dot(vt_ref[h, j], p, preferred_element_type=F32)

    buf_b, buf_a = s_bufs
    issue_scores(qi, buf_b)
    for h in range(hps):
        qm = jnp.where(lane_head == h, q, 0.0)
        gate = lax.dot_general(km, qm, nt, precision=HIGHEST, preferred_element_type=F32)
        bias_refs[h][...] = _select_blocks(gate, qi, axis=0)
        m_refs[h][...] = jnp.full(m_refs[h].shape, NEG_INF, F32)
        acc_refs[h][...] = jnp.zeros(acc_refs[h].shape, F32)
    issue_scores(0, buf_a)
    consume(qi, buf_b, own=True)

    def quad(i, carry):
        j = 4 * i
        issue_scores(j + 1, buf_b)
        consume(j, buf_a)
        issue_scores(j + 2, buf_a)
        consume(j + 1, buf_b)
        issue_scores(j + 3, buf_b)
        consume(j + 2, buf_a)
        issue_scores(j + 4, buf_a)
        consume(j + 3, buf_b)
        return carry

    lax.fori_loop(0, qi // 4, quad, 0)
    j0 = (qi // 4) * 4
    rem = qi - j0

    @pl.when(rem >= 1)
    def _():
        issue_scores(j0 + 1, buf_b)
        consume(j0, buf_a)

    @pl.when(rem >= 2)
    def _():
        issue_scores(j0 + 2, buf_a)
        consume(j0 + 1, buf_b)

    @pl.when(rem == 3)
    def _():
        consume(j0 + 2, buf_a)

    out_t = jnp.concatenate(
        [acc_refs[h][:HEAD_DIM, :] / acc_refs[h][HEAD_DIM:HEAD_DIM + 1, :] for h in range(hps)], axis=0)
    o_ref[...] = out_t.T


def _moba_prompt(q, kb, vt, km, batch, seq_len):
    t = q.shape[0]
    nb = seq_len // MOBA_BLOCK
    tq = MOBA_BLOCK
    hps = HEADS_PER_STEP
    gw = hps * HEAD_DIM
    scratch = ([pltpu.VMEM((tq, gw), BF16) for _ in range(hps)]
               + [pltpu.VMEM((nb, tq), F32) for _ in range(hps)]
               + [pltpu.VMEM((1, tq), F32) for _ in range(hps)]
               + [pltpu.VMEM((VT_ROWS, tq), F32) for _ in range(hps)]
               + [pltpu.VMEM((MOBA_BLOCK, tq), F32) for _ in range(2 * hps)])
    return pl.pallas_call(
        _moba_prompt_kernel,
        grid=(batch, N_HEADS // hps, nb),
        in_specs=[
            pl.BlockSpec((tq, gw), lambda b, g, i: (b * nb + i, g)),
            pl.BlockSpec((seq_len, gw), lambda b, g, i: (b, g)),
            pl.BlockSpec((hps, nb, VT_ROWS, MOBA_BLOCK), lambda b, g, i: (g, b, 0, 0)),
            pl.BlockSpec((1, nb, gw), lambda b, g, i: (b, 0, g)),
        ],
        out_specs=pl.BlockSpec((tq, gw), lambda b, g, i: (b * nb + i, g)),
        out_shape=jax.ShapeDtypeStruct((t, ATTN_WIDTH), F32),
        scratch_shapes=scratch,
        compiler_params=_params("parallel", "parallel", "arbitrary"),
        name="moba_prompt",
    )(q, kb, vt, km)


SEQS_PER_STEP = 2


def _moba_sample_kernel(n_pages, pt_ref, q_ref, kn_ref, vn_ref, *rest):
    del pt_ref
    o_ref = rest[-1]
    for s in range(q_ref.shape[0]):
        pages = rest[2 * n_pages * s:2 * n_pages * (s + 1)]
        o_ref[s] = _moba_sample_one(q_ref[s], kn_ref[s], vn_ref[s], pages[:n_pages], pages[n_pages:])


def _moba_sample_one(q, k_new, v_new, k_pages, v_pages):
    n_pages = len(k_pages)
    nq = q.shape[0]
    rows = N_HEADS * nq
    ppb = MOBA_BLOCK // PAGE_SIZE
    nb_past = n_pages // ppb
    nt = (((1,), (1,)), ((), ()))

    q_rep = jnp.concatenate([q] * N_HEADS, axis=0)
    r_i = lax.broadcasted_iota(jnp.int32, (rows, ATTN_WIDTH), 0)
    c_i = lax.broadcasted_iota(jnp.int32, (rows, ATTN_WIDTH), 1)
    head_mask = (c_i // HEAD_DIM) == (r_i // nq)
    qmat = jnp.where(head_mask, q_rep, 0.0)
    qmat_b = qmat.astype(BF16)

    def page_t(ref):
        return ref[0].reshape(ATTN_WIDTH, PAGE_SIZE)

    blk_lane = lax.broadcasted_iota(jnp.int32, (ATTN_WIDTH, nb_past), 1)
    kmean_t = jnp.zeros((ATTN_WIDTH, nb_past), F32)
    for n in range(nb_past):
        tot = page_t(k_pages[n * ppb])
        for pg in range(n * ppb + 1, (n + 1) * ppb):
            tot = tot + page_t(k_pages[pg])
        col = jnp.sum(tot, axis=1, keepdims=True) * (1.0 / MOBA_BLOCK)
        kmean_t = jnp.where(blk_lane == n, col, kmean_t)
    gate = jnp.dot(qmat, kmean_t, precision=HIGHEST, preferred_element_type=F32)
    bias = _select_blocks(gate, nb_past, axis=1)

    s_parts = []
    for pg in range(n_pages):
        sp = jnp.dot(qmat_b, page_t(k_pages[pg]).astype(BF16), preferred_element_type=F32)
        n = pg // ppb
        s_parts.append(sp + bias[:, n:n + 1])
    sn = lax.dot_general(qmat_b, k_new.astype(BF16), nt, preferred_element_type=F32)
    key_i = lax.broadcasted_iota(jnp.int32, (rows, nq), 1)
    qry_i = lax.broadcasted_iota(jnp.int32, (rows, nq), 0) % nq
    sn = jnp.where(key_i <= qry_i, sn, NEG_INF)

    m_el = s_parts[0]
    for sp in s_parts[1:]:
        m_el = jnp.maximum(m_el, sp)
    m = jnp.maximum(jnp.max(m_el, axis=1, keepdims=True), jnp.max(sn, axis=1, keepdims=True))
    pn = jnp.exp(sn - m)
    acc = jnp.dot(pn.astype(BF16), v_new.astype(BF16), preferred_element_type=F32)
    l_el = jnp.zeros((rows, PAGE_SIZE), F32)
    for pg in range(n_pages):
        p = jnp.exp(s_parts[pg] - m)
        l_el = l_el + p
        acc = acc + lax.dot_general(p.astype(BF16), page_t(v_pages[pg]).astype(BF16), nt,
                                    preferred_element_type=F32)
    l = jnp.sum(pn, axis=1, keepdims=True) + jnp.sum(l_el, axis=1, keepdims=True)
    out = jnp.where(head_mask, acc / l, 0.0)
    res = out[0:nq]
    for h in range(1, N_HEADS):
        res = res + out[h * nq:(h + 1) * nq]
    return res


def _moba_sample(q, k_new, v_new, cache_k, cache_v, page_table, layer, n_pool):
    bsz, nq, _ = q.shape
    n_pages = page_table.shape[1]
    base = layer * n_pool
    spp = SEQS_PER_STEP if bsz % SEQS_PER_STEP == 0 else 1
    tok = pl.BlockSpec((spp, nq, ATTN_WIDTH), lambda b, pt: (b, 0, 0))

    def page_spec(s, pg):
        return pl.BlockSpec((1, N_HEADS, HEAD_DIM, PAGE_SIZE),
                            lambda b, pt: (base + pt[b * spp + s, pg], 0, 0, 0))

    page_specs, page_args = [], []
    for s in range(spp):
        page_specs += [page_spec(s, pg) for pg in range(n_pages)] * 2
        page_args += [cache_k] * n_pages + [cache_v] * n_pages
    grid_spec = pltpu.PrefetchScalarGridSpec(
        num_scalar_prefetch=1,
        grid=(bsz // spp,),
        in_specs=[tok, tok, tok] + page_specs,
        out_specs=tok,
    )
    return pl.pallas_call(
        functools.partial(_moba_sample_kernel, n_pages),
        grid_spec=grid_spec,
        out_shape=jax.ShapeDtypeStruct((bsz, nq, ATTN_WIDTH), F32),
        compiler_params=_params("parallel"),
        name="moba_sample",
    )(page_table, q, k_new, v_new, *page_args)


def _s5_discretise(a_re, a_im, log_dt, b_re, b_im):
    dt = jnp.exp(log_dt.astype(F32))[:, None]
    ar, ai = a_re.astype(F32), a_im.astype(F32)
    mag = jnp.exp(ar * dt)
    lr, li = mag * jnp.cos(ai * dt), mag * jnp.sin(ai * dt)
    den = ar * ar + ai * ai
    nr = lr - 1.0
    fr = (nr * ar + li * ai) / den
    fi = (li * ar - nr * ai) / den
    br_, bi_ = b_re.astype(F32), b_im.astype(F32)
    bbr = fr[..., None] * br_ - fi[..., None] * bi_
    bbi = fr[..., None] * bi_ + fi[..., None] * br_
    return ar * dt, ai * dt, lr, li, bbr, bbi


def _s5_prompt_mats(a_re, a_im, log_dt, b_re, b_im, c_re, c_im):
    c = SSM_CHUNK
    g_, p_, cdim = SSM_GROUPS, SSM_STATE, SSM_GROUP_DIM
    adr, adi, _, _, bbr, bbi = _s5_discretise(a_re, a_im, log_dt, b_re, b_im)
    tau = jnp.arange(c + 1, dtype=F32)[:, None, None]
    pmag = jnp.exp(adr[None] * tau)
    pw_r, pw_i = pmag * jnp.cos(adi[None] * tau), pmag * jnp.sin(adi[None] * tau)
    cr, ci = c_re.astype(F32), c_im.astype(F32)
    lb_r = pw_r[..., None] * bbr[None] - pw_i[..., None] * bbi[None]
    lb_i = pw_r[..., None] * bbi[None] + pw_i[..., None] * bbr[None]
    kern = (jnp.einsum('gop,tgpi->tgoi', cr, lb_r[:c], precision=HIGHEST)
            - jnp.einsum('gop,tgpi->tgoi', ci, lb_i[:c], precision=HIGHEST))
    s_i = jnp.arange(c)[:, None]
    t_i = jnp.arange(c)[None, :]
    lag = jnp.where(t_i >= s_i, t_i - s_i, c)
    kern_z = jnp.concatenate([kern.transpose(0, 1, 3, 2), jnp.zeros((1, g_, cdim, cdim), F32)]).astype(BF16)
    toep = kern_z[lag]
    toep = toep.transpose(2, 0, 3, 1, 4).reshape(g_, c * cdim, c * cdim)
    w_r = lb_r[:c][::-1].transpose(1, 0, 3, 2).reshape(g_, c * cdim, p_)
    w_i = lb_i[:c][::-1].transpose(1, 0, 3, 2).reshape(g_, c * cdim, p_)
    pr, pi = pw_r[1:], pw_i[1:]
    v_r = cr[None] * pr[:, :, None, :] - ci[None] * pi[:, :, None, :]
    v_i = -cr[None] * pi[:, :, None, :] - ci[None] * pr[:, :, None, :]
    v_r = v_r.transpose(1, 3, 0, 2).reshape(g_, p_, c * cdim)
    v_i = v_i.transpose(1, 3, 0, 2).reshape(g_, p_, c * cdim)
    decay = jnp.stack([pw_r[c], pw_i[c]], axis=1)
    return toep, w_r.astype(BF16), w_i.astype(BF16), v_r.astype(BF16), v_i.astype(BF16), decay


def _s5_sample_mats(a_re, a_im, log_dt, b_re, b_im, c_re, c_im):
    g_, p_, cdim = SSM_GROUPS, SSM_STATE, SSM_GROUP_DIM
    _, _, lr, li, bbr, bbi = _s5_discretise(a_re, a_im, log_dt, b_re, b_im)
    eye = jnp.eye(g_, dtype=F32)
    b_bd_r = (eye[:, None, :, None] * bbr.transpose(0, 2, 1)[:, :, None, :]).reshape(g_ * cdim, g_ * p_)
    b_bd_i = (eye[:, None, :, None] * bbi.transpose(0, 2, 1)[:, :, None, :]).reshape(g_ * cdim, g_ * p_)
    cr, ci = c_re.astype(F32), c_im.astype(F32)
    c_bd_r = (eye[:, None, :, None] * cr.transpose(0, 2, 1)[:, :, None, :]).reshape(g_ * p_, g_ * cdim)
    c_bd_i = (eye[:, None, :, None] * ci.transpose(0, 2, 1)[:, :, None, :]).reshape(g_ * p_, g_ * cdim)
    return lr.reshape(1, g_ * p_), li.reshape(1, g_ * p_), b_bd_r, b_bd_i, c_bd_r, c_bd_i


def _s5_prompt_kernel(n_batch, u_ref, toep_ref, wr_ref, wi_ref, vr_ref, vi_ref, dec_ref,
                      y_ref, hr_ref, hi_ref, sr_s, si_s, pr_s, pi_s):
    u = u_ref[0]
    n_chunks = u.shape[0]
    cpb = n_chunks // n_batch
    sr_s[...] = jnp.dot(u, wr_ref[0], preferred_element_type=F32)
    si_s[...] = jnp.dot(u, wi_ref[0], preferred_element_type=F32)
    dr = dec_ref[0, 0:1, :]
    di = dec_ref[0, 1:2, :]

    def step(kk, carry):
        new = []
        for b in range(n_batch):
            hr, hi = carry[2 * b], carry[2 * b + 1]
            r = b * cpb + kk
            pr_s[pl.ds(r, 1), :] = hr
            pi_s[pl.ds(r, 1), :] = hi
            new.append(dr * hr - di * hi + sr_s[pl.ds(r, 1), :])
            new.append(dr * hi + di * hr + si_s[pl.ds(r, 1), :])
        return tuple(new)

    zero = jnp.zeros((1, SSM_STATE), F32)
    fin = lax.fori_loop(0, cpb, step, (zero,) * (2 * n_batch))
    for b in range(n_batch):
        hr_ref[0, b:b + 1, :] = fin[2 * b]
        hi_ref[0, b:b + 1, :] = fin[2 * b + 1]
    y = jnp.dot(u, toep_ref[0], preferred_element_type=F32)
    y = y + jnp.dot(pr_s[...].astype(BF16), vr_ref[0], preferred_element_type=F32)
    y = y + jnp.dot(pi_s[...].astype(BF16), vi_ref[0], preferred_element_type=F32)
    y_ref[0] = y


def _s5_prompt(u_g, mats, n_batch):
    toep, w_r, w_i, v_r, v_i, decay = mats
    g_, n_chunks, cw = u_g.shape
    per_g = lambda *shape: pl.BlockSpec((1,) + shape, lambda g: (g,) + (0,) * len(shape))
    st = jax.ShapeDtypeStruct((g_, n_batch, SSM_STATE), F32)
    return pl.pallas_call(
        functools.partial(_s5_prompt_kernel, n_batch),
        grid=(g_,),
        in_specs=[per_g(n_chunks, cw), per_g(cw, cw), per_g(cw, SSM_STATE), per_g(cw, SSM_STATE),
                  per_g(SSM_STATE, cw), per_g(SSM_STATE, cw), per_g(2, SSM_STATE)],
        out_specs=(per_g(n_chunks, cw), per_g(n_batch, SSM_STATE), per_g(n_batch, SSM_STATE)),
        out_shape=(jax.ShapeDtypeStruct((g_, n_chunks, cw), F32), st, st),
        scratch_shapes=[pltpu.VMEM((n_chunks, SSM_STATE), F32) for _ in range(4)],
        compiler_params=_params("parallel"),
        name="s5_prompt",
    )(u_g, toep, w_r, w_i, v_r, v_i, decay)


def _s5_sample_kernel(u_ref, h0r_ref, h0i_ref, lr_ref, li_ref, br_ref, bi_ref, cr_ref, ci_ref,
                      y_ref, hr_ref, hi_ref):
    n_steps = u_ref.shape[0]
    lr, li = lr_ref[...], li_ref[...]
    hr, hi = h0r_ref[...], h0i_ref[...]
    for t in range(n_steps):
        u = u_ref[t]
        bur = jnp.dot(u, br_ref[...], precision=HIGHEST, preferred_element_type=F32)
        bui = jnp.dot(u, bi_ref[...], precision=HIGHEST, preferred_element_type=F32)
        hr, hi = lr * hr - li * hi + bur, lr * hi + li * hr + bui
        y_ref[t] = (jnp.dot(hr, cr_ref[...], precision=HIGHEST, preferred_element_type=F32)
                    - jnp.dot(hi, ci_ref[...], precision=HIGHEST, preferred_element_type=F32))
    hr_ref[...] = hr
    hi_ref[...] = hi


def _s5_sample(u_t, h0r, h0i, mats):
    n_steps, bsz, _ = u_t.shape
    st = jax.ShapeDtypeStruct(h0r.shape, F32)
    return pl.pallas_call(
        _s5_sample_kernel,
        out_shape=(jax.ShapeDtypeStruct((n_steps, bsz, SSM_WIDTH), F32), st, st),
        compiler_params=pltpu.CompilerParams(vmem_limit_bytes=VMEM_LIMIT),
        name="s5_sample",
    )(u_t, h0r, h0i, *mats)


def _layer_norm(x, g, b):
    mu = jnp.mean(x, axis=-1, keepdims=True)
    xc = x - mu
    var = jnp.mean(xc * xc, axis=-1, keepdims=True)
    return xc * lax.rsqrt(var + LN_EPS) * g + b


def _rms_gain(y, g):
    return y * lax.rsqrt(jnp.mean(y * y, axis=-1, keepdims=True) + RMS_EPS) * g


def _sigmoid(x):
    return 1.0 / (1.0 + jnp.exp(-x))


def _gelu_tanh(x):
    return 0.5 * x * (1.0 + jnp.tanh(math.sqrt(2.0 / math.pi) * (x + 0.044715 * (x * x * x))))


def _route(lg, le):
    tm = lg.shape[0]
    gi = lax.broadcasted_iota(jnp.int32, lg.shape, 1)
    gmax = jnp.max(lg, axis=-1, keepdims=True)
    g_sel = jnp.min(jnp.where(lg == gmax, gi, N_GROUPS), axis=-1, keepdims=True)
    p_sel = 1.0 / jnp.sum(jnp.exp(lg - gmax), axis=-1, keepdims=True)
    ei = lax.broadcasted_iota(jnp.int32, (tm, N_EXPERTS), 1)
    cand = jnp.where((ei // EXPERTS_PER_GROUP) == g_sel, le, -jnp.inf)
    v0 = jnp.max(cand, axis=-1, keepdims=True)
    i0 = jnp.min(jnp.where(cand == v0, ei, N_EXPERTS), axis=-1, keepdims=True)
    cand = jnp.where(ei == i0, -jnp.inf, cand)
    v1 = jnp.max(cand, axis=-1, keepdims=True)
    i1 = jnp.min(jnp.where(cand == v1, ei, N_EXPERTS), axis=-1, keepdims=True)
    e1 = jnp.exp(v1 - v0)
    w0 = p_sel / (1.0 + e1)
    w1 = p_sel * e1 / (1.0 + e1)
    return jnp.where(ei == i0, w0, 0.0) + jnp.where(ei == i1, w1, 0.0)


def _post_mixer_kernel(y_ref, u_ref, a_ref, x_ref, d_ref, wglu_ref, bglu_ref, ga_ref, gs_ref,
                       wout_ref, g1_ref, b1_ref, wr_ref, br_ref, x1_ref, comb_ref):
    tm = x_ref.shape[0]
    sub = min(tm, POST_MIXER_SUB_ROWS)
    for r in range(tm // sub):
        rows = pl.ds(r * sub, sub)
        y = y_ref[rows, :] + d_ref[...] * u_ref[rows, :]
        g = _gelu_tanh(y)
        z = g * _sigmoid(jnp.dot(g.astype(BF16), wglu_ref[...], preferred_element_type=F32) + bglu_ref[...])
        mix_in = jnp.concatenate([_rms_gain(a_ref[rows, :], ga_ref[...]), _rms_gain(z, gs_ref[...])], axis=-1)
        mixed = jnp.dot(mix_in.astype(BF16), wout_ref[...], preferred_element_type=F32)
        x1 = _layer_norm(ALPHA * x_ref[rows, :] + mixed, g1_ref[...], b1_ref[...])
        x1_ref[rows, :] = x1
        logits = jnp.dot(x1, wr_ref[...], precision=HIGHEST, preferred_element_type=F32) + br_ref[...]
        comb_ref[rows, :] = _route(logits[:, :N_GROUPS], logits[:, N_GROUPS:N_GROUPS + N_EXPERTS])


def _post_mixer(y, u, attn, x, lp, tm):
    t = x.shape[0]
    tm = min(tm, t)
    row = lambda i: (i, 0)
    full = lambda a: pl.BlockSpec(a.shape, lambda i: (0, 0))
    weights = [lp['d'], lp['w_glu'], lp['b_glu'], lp['g_attn'], lp['g_ssm'], lp['w_out'],
               lp['ln1_g'], lp['ln1_b'], lp['w_router'], lp['b_router']]
    return pl.pallas_call(
        _post_mixer_kernel,
        grid=(t // tm,),
        in_specs=[pl.BlockSpec((tm, SSM_WIDTH), row), pl.BlockSpec((tm, SSM_WIDTH), row),
                  pl.BlockSpec((tm, ATTN_WIDTH), row), pl.BlockSpec((tm, D_MODEL), row)]
                 + [full(w) for w in weights],
        out_specs=(pl.BlockSpec((tm, D_MODEL), row), pl.BlockSpec((tm, N_EXPERTS), row)),
        out_shape=(jax.ShapeDtypeStruct((t, D_MODEL), F32), jax.ShapeDtypeStruct((t, N_EXPERTS), F32)),
        compiler_params=_params("parallel"),
        name="post_mixer",
    )(y, u, attn, x, *weights)


def _moe_kernel(x_ref, comb_ref, wg_ref, wu_ref, wd_ref, g2_ref, b2_ref, o_ref, acc_ref):
    e = pl.program_id(1)

    @pl.when(e == 0)
    def _():
        acc_ref[...] = jnp.zeros_like(acc_ref)

    xb = x_ref[...].astype(BF16)
    hg = jnp.dot(xb, wg_ref[0].astype(BF16), preferred_element_type=F32)
    hu = jnp.dot(xb, wu_ref[0].astype(BF16), preferred_element_type=F32)
    h = hg * _sigmoid(hg) * hu
    y = jnp.dot(h.astype(BF16), wd_ref[0].astype(BF16), preferred_element_type=F32)
    comb = comb_ref[...]
    ei = lax.broadcasted_iota(jnp.int32, comb.shape, 1)
    w = jnp.sum(jnp.where(ei == e, comb, 0.0), axis=-1, keepdims=True)
    acc_ref[...] += w * y

    @pl.when(e == pl.num_programs(1) - 1)
    def _():
        o_ref[...] = _layer_norm(ALPHA * x_ref[...] + acc_ref[...], g2_ref[...], b2_ref[...])


def _moe(x1, comb, lp, tm):
    t = x1.shape[0]
    layer = lp['layer']
    row = lambda i, e: (i, 0)
    return pl.pallas_call(
        _moe_kernel,
        grid=(t // tm, N_EXPERTS),
        in_specs=[
            pl.BlockSpec((tm, D_MODEL), row),
            pl.BlockSpec((tm, N_EXPERTS), row),
            pl.BlockSpec((None, 1, D_MODEL, D_EXPERT), lambda i, e: (layer, e, 0, 0)),
            pl.BlockSpec((None, 1, D_MODEL, D_EXPERT), lambda i, e: (layer, e, 0, 0)),
            pl.BlockSpec((None, 1, D_EXPERT, D_MODEL), lambda i, e: (layer, e, 0, 0)),
            pl.BlockSpec((1, D_MODEL), lambda i, e: (0, 0)),
            pl.BlockSpec((1, D_MODEL), lambda i, e: (0, 0)),
        ],
        out_specs=pl.BlockSpec((tm, D_MODEL), row),
        out_shape=jax.ShapeDtypeStruct((t, D_MODEL), F32),
        scratch_shapes=[pltpu.VMEM((tm, D_MODEL), F32)],
        compiler_params=_params("parallel", "arbitrary"),
        name="moe",
    )(x1, comb, lp['w_gate'], lp['w_up'], lp['w_down'], lp['ln2_g'], lp['ln2_b'])


def _prompt_layer(x, lp, batch, seq_len, cos_t, sin_t, layer, depth, kbuf, vbuf):
    t = x.shape[0]
    nb = seq_len // MOBA_BLOCK
    kbuf, vbuf, u, q, kb, vt, km = _inproj_prompt(x, lp['w_in'], cos_t, sin_t, seq_len,
                                                   layer, depth, kbuf, vbuf)
    attn = _moba_prompt(q, kb, vt, km.reshape(batch, nb, ATTN_WIDTH), batch, seq_len)
    c = SSM_CHUNK
    u_g = (u.astype(BF16).reshape(t // c, c, SSM_GROUPS, SSM_GROUP_DIM)
           .transpose(2, 0, 1, 3).reshape(SSM_GROUPS, t // c, c * SSM_GROUP_DIM))
    y_g, hr, hi = _s5_prompt(u_g, lp['s5_prompt'], batch)
    y = (y_g.reshape(SSM_GROUPS, t // c, c, SSM_GROUP_DIM)
         .transpose(1, 2, 0, 3).reshape(t, SSM_WIDTH))
    x1, comb = _post_mixer(y, u, attn, x, lp, tm=512)
    x2 = _moe(x1, comb, lp, tm=1024)
    return x2, kbuf, vbuf, hr.transpose(1, 0, 2), hi.transpose(1, 0, 2)


def _sample_layer(x, lp, bsz, nq, cos_t, sin_t, cache_k, cache_v, page_table, layer, n_pool,
                  h0r, h0i):
    t = x.shape[0]
    q, k, v, u = _inproj_sample(x, lp['w_in'], cos_t, sin_t)
    to3 = lambda a: a.reshape(bsz, nq, ATTN_WIDTH)
    attn = _moba_sample(to3(q), to3(k), to3(v), cache_k, cache_v, page_table, layer, n_pool)
    attn = attn.reshape(t, ATTN_WIDTH)
    u_t = u.reshape(bsz, nq, SSM_WIDTH).transpose(1, 0, 2)
    y_t, hr, hi = _s5_sample(u_t, h0r, h0i, lp['s5_sample'])
    y = y_t.transpose(1, 0, 2).reshape(t, SSM_WIDTH)
    x1, comb = _post_mixer(y, u, attn, x, lp, tm=256)
    x2 = _moe(x1, comb, lp, tm=t)
    return x2, k, v, hr, hi


def _all_layer_params(w_in, w_out, norm_attn_g, norm_ssm_g, ssm_a_re, ssm_a_im, ssm_log_dt,
                      ssm_b_re, ssm_b_im, ssm_c_re, ssm_c_im, ssm_d, w_glu, b_glu, ln1_g, ln1_b,
                      w_group, b_group, w_expert, b_expert, w_gate, w_up, w_down, ln2_g, ln2_b):
    depth = w_in.shape[0]
    rowv = lambda a: a.reshape(depth, 1, -1).astype(F32)
    s5_args = (ssm_a_re, ssm_a_im, ssm_log_dt, ssm_b_re, ssm_b_im, ssm_c_re, ssm_c_im)
    pad = LANES - N_GROUPS - N_EXPERTS
    w_router = jnp.concatenate([w_group, w_expert, jnp.zeros((depth, D_MODEL, pad), F32)], axis=2)
    b_router = jnp.concatenate([b_group, b_expert, jnp.zeros((depth, pad), F32)], axis=1)
    return dict(
        w_in=w_in.astype(BF16), w_out=w_out.astype(BF16), w_glu=w_glu.astype(BF16),
        g_attn=rowv(norm_attn_g), g_ssm=rowv(norm_ssm_g), d=rowv(ssm_d), b_glu=rowv(b_glu),
        ln1_g=rowv(ln1_g), ln1_b=rowv(ln1_b), ln2_g=rowv(ln2_g), ln2_b=rowv(ln2_b),
        w_router=w_router.astype(F32), b_router=rowv(b_router),
        s5_prompt=jax.vmap(_s5_prompt_mats)(*s5_args), s5_sample=jax.vmap(_s5_sample_mats)(*s5_args),
    ), dict(w_gate=w_gate, w_up=w_up, w_down=w_down)


def _layer_params(stacked, experts, l):
    lp = jax.tree.map(lambda a: a[l], stacked)
    return dict(lp, layer=l, **experts)


def kernel(x_prompt, x_sample, cache_k, cache_v, state_ssm_re, state_ssm_im, page_table, w_in, w_out, norm_attn_g, norm_ssm_g, ssm_a_re, ssm_a_im, ssm_log_dt, ssm_b_re, ssm_b_im, ssm_c_re, ssm_c_im, ssm_d, w_glu, b_glu, ln1_g, ln1_b, w_group, b_group, w_expert, b_expert, w_gate, w_up, w_down, ln2_g, ln2_b):
    batch, seq_len, _ = x_prompt.shape
    dec_batch, dec_seq, _ = x_sample.shape
    depth, n_pool = cache_k.shape[0], cache_k.shape[1]
    n_pages = page_table.shape[1]
    past_len = n_pages * PAGE_SIZE
    assert seq_len % 512 == 0 and past_len % MOBA_BLOCK == 0 and dec_seq <= MOBA_BLOCK

    cos_p, sin_p = _rope_tables(jnp.arange(seq_len, dtype=jnp.int32))
    pos_s = past_len + jnp.arange(dec_seq, dtype=jnp.int32)
    cos_s, sin_s = _rope_tables(jnp.tile(pos_s, dec_batch))

    to_pages = lambda c: c.transpose(0, 1, 3, 4, 2).reshape(depth * n_pool, N_HEADS, HEAD_DIM, PAGE_SIZE)
    ck, cv = to_pages(cache_k), to_pages(cache_v)
    xp = x_prompt.reshape(batch * seq_len, D_MODEL)
    xs = x_sample.reshape(dec_batch * dec_seq, D_MODEL)
    kbuf = jnp.zeros((depth, batch, N_HEADS, HEAD_DIM, seq_len), F32)
    vbuf = jnp.zeros((depth, batch, N_HEADS, HEAD_DIM, seq_len), F32)
    outs = [[] for _ in range(6)]
    stacked, experts = _all_layer_params(
        w_in, w_out, norm_attn_g, norm_ssm_g, ssm_a_re, ssm_a_im, ssm_log_dt, ssm_b_re, ssm_b_im,
        ssm_c_re, ssm_c_im, ssm_d, w_glu, b_glu, ln1_g, ln1_b, w_group, b_group, w_expert, b_expert,
        w_gate, w_up, w_down, ln2_g, ln2_b)
    for l in range(depth):
        lp = _layer_params(stacked, experts, l)
        xp, kbuf, vbuf, hrp, hip = _prompt_layer(xp, lp, batch, seq_len, cos_p, sin_p, l, depth, kbuf, vbuf)
        h0r = state_ssm_re[l].reshape(dec_batch, SSM_GROUPS * SSM_STATE)
        h0i = state_ssm_im[l].reshape(dec_batch, SSM_GROUPS * SSM_STATE)
        xs, kn, vn, hrs, his = _sample_layer(xs, lp, dec_batch, dec_seq, cos_s, sin_s, ck, cv,
                                             page_table, l, n_pool, h0r, h0i)
        outs[0].append(hrp)
        outs[1].append(hip)
        outs[2].append(kn.reshape(dec_batch, dec_seq, N_HEADS, HEAD_DIM))
        outs[3].append(vn.reshape(dec_batch, dec_seq, N_HEADS, HEAD_DIM))
        outs[4].append(hrs.reshape(dec_batch, SSM_GROUPS, SSM_STATE))
        outs[5].append(his.reshape(dec_batch, SSM_GROUPS, SSM_STATE))
    return (xp.reshape(batch, seq_len, D_MODEL), xs.reshape(dec_batch, dec_seq, D_MODEL),
            kbuf.transpose(0, 1, 4, 2, 3), vbuf.transpose(0, 1, 4, 2, 3), *[jnp.stack(o) for o in outs])
```

```python
import functools
import math

import jax
import jax.numpy as jnp
from jax import lax
from jax.experimental import pallas as pl
from jax.experimental.pallas import tpu as pltpu

F32 = jnp.float32
BF16 = jnp.bfloat16
HIGHEST = lax.Precision.HIGHEST

D_MODEL = 1024
ATTN_WIDTH = 512
SSM_WIDTH = 512
HEAD_DIM = 64
N_HEADS = 8
MOBA_BLOCK = 256
MOBA_TOPK = 3
PAGE_SIZE = 128
ROPE_THETA = 10000.0
SSM_GROUP_DIM = 16
SSM_GROUPS = 32
SSM_STATE = 64
N_GROUPS = 4
EXPERTS_PER_GROUP = 4
N_EXPERTS = 16
D_EXPERT = 512
DEPTH = 4
ALPHA = (2.0 * DEPTH) ** 0.25
LN_EPS = 1e-5
RMS_EPS = 1e-6
NEG_INF = -1e30
LOG2E = 1.4426950408889634
PROJ_WIDTH = 3 * ATTN_WIDTH + SSM_WIDTH

LANES = 128
SSM_CHUNK = 32
VT_ROWS = HEAD_DIM + 16
POST_MIXER_SUB_ROWS = 256
VMEM_LIMIT = 48 * 1024 * 1024


def _params(*sem):
    return pltpu.CompilerParams(dimension_semantics=sem, vmem_limit_bytes=VMEM_LIMIT)


def _rope_tables(pos):
    inv_freq = 1.0 / jnp.power(ROPE_THETA, jnp.arange(0, HEAD_DIM, 2, dtype=F32) / HEAD_DIM)
    ang = pos.astype(F32)[:, None] * inv_freq[None, :]
    cos, sin = jnp.cos(ang), jnp.sin(ang)
    cos_t = jnp.concatenate([cos, cos, cos, cos], axis=-1)
    sin_t = jnp.concatenate([-sin, sin, -sin, sin], axis=-1)
    return cos_t, sin_t


def _rope(x, cos, sin):
    tm = x.shape[0]
    lane = lax.broadcasted_iota(jnp.int32, (tm, LANES), 1)
    first_half = (lane % HEAD_DIM) < (HEAD_DIM // 2)
    outs = []
    for c in range(x.shape[1] // LANES):
        xc = x[:, c * LANES:(c + 1) * LANES]
        partner = jnp.where(first_half, pltpu.roll(xc, LANES - HEAD_DIM // 2, 1),
                            pltpu.roll(xc, HEAD_DIM // 2, 1))
        outs.append(xc * cos + partner * sin)
    return jnp.concatenate(outs, axis=1)


def _inproj_prompt_kernel(x_ref, w_ref, cos_ref, sin_ref, *rest):
    k_ref, v_ref, u_ref, q_ref, kb_ref, vt_ref, km_ref = rest[-7:]
    tm = x_ref.shape[0]
    proj = jnp.dot(x_ref[...].astype(BF16), w_ref[...], preferred_element_type=F32)
    cos, sin = cos_ref[...], sin_ref[...]
    q = _rope(proj[:, :ATTN_WIDTH], cos, sin)
    k = _rope(proj[:, ATTN_WIDTH:2 * ATTN_WIDTH], cos, sin)
    v = proj[:, 2 * ATTN_WIDTH:3 * ATTN_WIDTH]
    u_ref[...] = proj[:, 3 * ATTN_WIDTH:]
    q_ref[...] = q * (HEAD_DIM ** -0.5 * LOG2E)
    kb_ref[...] = k.astype(BF16)
    vt = v.T
    k_ref[0, 0] = k.T.reshape(N_HEADS, HEAD_DIM, tm)
    v_ref[0, 0] = vt.reshape(N_HEADS, HEAD_DIM, tm)
    pad_row = lax.broadcasted_iota(jnp.int32, (N_HEADS, VT_ROWS - HEAD_DIM, MOBA_BLOCK), 1)
    ones_pad = jnp.where(pad_row == 0, 1.0, 0.0).astype(BF16)
    for blk in range(tm // MOBA_BLOCK):
        sl = slice(blk * MOBA_BLOCK, (blk + 1) * MOBA_BLOCK)
        vt_blk = vt[:, sl].reshape(N_HEADS, HEAD_DIM, MOBA_BLOCK).astype(BF16)
        vt_ref[:, blk] = jnp.concatenate([vt_blk, ones_pad], axis=1)
        km_ref[blk] = jnp.mean(k[sl], axis=0, keepdims=True)


def _inproj_prompt(x, w_in, cos_t, sin_t, seq_len, layer, depth, kbuf, vbuf, tm=512):
    t = x.shape[0]
    n_tiles_seq = seq_len // tm
    nblk = t // MOBA_BLOCK
    bpt = tm // MOBA_BLOCK
    row = lambda i: (i, 0)
    kv_sds = jax.ShapeDtypeStruct((depth, t // seq_len, N_HEADS, HEAD_DIM, seq_len), F32)
    out_shape = (
        kv_sds,
        kv_sds,
        jax.ShapeDtypeStruct((t, SSM_WIDTH), F32),
        jax.ShapeDtypeStruct((t, ATTN_WIDTH), F32),
        jax.ShapeDtypeStruct((t, ATTN_WIDTH), BF16),
        jax.ShapeDtypeStruct((N_HEADS, nblk, VT_ROWS, MOBA_BLOCK), BF16),
        jax.ShapeDtypeStruct((nblk, 1, ATTN_WIDTH), F32),
    )
    kv_spec = pl.BlockSpec((1, 1, N_HEADS, HEAD_DIM, tm),
                           lambda i: (layer, i // n_tiles_seq, 0, 0, i % n_tiles_seq))
    any_spec = pl.BlockSpec(memory_space=pl.ANY)
    in_specs = [
        pl.BlockSpec((tm, D_MODEL), row),
        pl.BlockSpec((D_MODEL, PROJ_WIDTH), lambda i: (0, 0)),
        pl.BlockSpec((tm, LANES), lambda i: (i % n_tiles_seq, 0)),
        pl.BlockSpec((tm, LANES), lambda i: (i % n_tiles_seq, 0)),
    ]
    out_specs = (
        kv_spec, kv_spec,
        pl.BlockSpec((tm, SSM_WIDTH), row),
        pl.BlockSpec((tm, ATTN_WIDTH), row),
        pl.BlockSpec((tm, ATTN_WIDTH), row),
        pl.BlockSpec((N_HEADS, bpt, VT_ROWS, MOBA_BLOCK), lambda i: (0, i, 0, 0)),
        pl.BlockSpec((bpt, 1, ATTN_WIDTH), lambda i: (i, 0, 0)),
    )
    return pl.pallas_call(
        _inproj_prompt_kernel,
        grid=(t // tm,),
        in_specs=in_specs + [any_spec, any_spec],
        out_specs=out_specs,
        out_shape=out_shape,
        input_output_aliases={4: 0, 5: 1},
        compiler_params=_params("parallel"),
        name="inproj_prompt",
    )(x, w_in, cos_t, sin_t, kbuf, vbuf)


def _inproj_sample_kernel(x_ref, w_ref, cos_ref, sin_ref, q_ref, k_ref, v_ref, u_ref):
    proj = jnp.dot(x_ref[...].astype(BF16), w_ref[...], preferred_element_type=F32)
    cos, sin = cos_ref[...], sin_ref[...]
    q_ref[...] = _rope(proj[:, :ATTN_WIDTH], cos, sin) * (HEAD_DIM ** -0.5)
    k_ref[...] = _rope(proj[:, ATTN_WIDTH:2 * ATTN_WIDTH], cos, sin)
    v_ref[...] = proj[:, 2 * ATTN_WIDTH:3 * ATTN_WIDTH]
    u_ref[...] = proj[:, 3 * ATTN_WIDTH:]


def _inproj_sample(x, w_in, cos_t, sin_t, tm=256):
    t = x.shape[0]
    tm = min(tm, t)
    row = lambda i: (i, 0)
    sds = jax.ShapeDtypeStruct((t, ATTN_WIDTH), F32)
    return pl.pallas_call(
        _inproj_sample_kernel,
        grid=(t // tm,),
        in_specs=[
            pl.BlockSpec((tm, D_MODEL), row),
            pl.BlockSpec((D_MODEL, PROJ_WIDTH), lambda i: (0, 0)),
            pl.BlockSpec((tm, LANES), row),
            pl.BlockSpec((tm, LANES), row),
        ],
        out_specs=tuple(pl.BlockSpec((tm, ATTN_WIDTH), row) for _ in range(4)),
        out_shape=(sds, sds, sds, sds),
        compiler_params=_params("parallel"),
        name="inproj_sample",
    )(x, w_in, cos_t, sin_t)


def _select_blocks(gate, n_valid, axis):
    nb = gate.shape[axis]
    blk = lax.broadcasted_iota(jnp.int32, gate.shape, axis)
    gate = jnp.where(blk < n_valid, gate, NEG_INF)
    sel = jnp.zeros(gate.shape, dtype=jnp.bool_)
    for r in range(MOBA_TOPK):
        top = jnp.max(gate, axis=axis, keepdims=True)
        first = jnp.min(jnp.where(gate == top, blk, nb), axis=axis, keepdims=True)
        hit = blk == first
        sel = jnp.logical_or(sel, jnp.logical_and(hit, r < n_valid))
        gate = jnp.where(hit, -jnp.inf, gate)
    return jnp.where(sel, 0.0, NEG_INF).astype(F32)


HEADS_PER_STEP = 4


def _moba_prompt_kernel(q_ref, k_ref, vt_ref, km_ref, o_ref, *scratch):
    hps = HEADS_PER_STEP
    qm_refs, bias_refs = scratch[:hps], scratch[hps:2 * hps]
    m_refs, acc_refs = scratch[2 * hps:3 * hps], scratch[3 * hps:4 * hps]
    s_bufs = (scratch[4 * hps:5 * hps], scratch[5 * hps:6 * hps])
    qi = pl.program_id(2)
    q = q_ref[...]
    lane_head = lax.broadcasted_iota(jnp.int32, q.shape, 1) // HEAD_DIM
    km = km_ref[0]
    nt = (((1,), (1,)), ((), ()))
    for h in range(hps):
        qm_refs[h][...] = jnp.where(lane_head == h, q, 0.0).astype(BF16)

    def issue_scores(j, bufs):
        k_j = k_ref[pl.ds(pl.multiple_of(j * MOBA_BLOCK, MOBA_BLOCK), MOBA_BLOCK), :]
        for h in range(hps):
            bufs[h][...] = lax.dot_general(k_j, qm_refs[h][...], nt, preferred_element_type=F32)

    def consume(j, bufs, own=False):
        for h in range(hps):
            s = bufs[h][...]
            if own:
                key_i = lax.broadcasted_iota(jnp.int32, s.shape, 0)
                qry_i = lax.broadcasted_iota(jnp.int32, s.shape, 1)
                s = jnp.where(key_i <= qry_i, s, NEG_INF)
                bias = jnp.zeros((1, s.shape[1]), F32)
            else:
                bias = bias_refs[h][pl.ds(j, 1), :]
            m = m_refs[h][...]
            m_new = jnp.maximum(m, jnp.max(s, axis=0, keepdims=True) + bias)
            a = jnp.exp2(m - m_new)
            p = jnp.exp2((s + (bias - m_new)).astype(BF16))
            m_refs[h][...] = m_new
            acc_refs[h][...] = a * acc_refs[h][...] + jnp.dot(vt_ref[h, j], p, preferred_element_type=F32)

    buf_b, buf_a = s_bufs
    issue_scores(qi, buf_b)
    for h in range(hps):
        qm = jnp.where(lane_head == h, q, 0.0)
        gate = lax.dot_general(km, qm, nt, precision=HIGHEST, preferred_element_type=F32)
        bias_refs[h][...] = _select_blocks(gate, qi, axis=0)
        m_refs[h][...] = jnp.full(m_refs[h].shape, NEG_INF, F32)
        acc_refs[h][...] = jnp.zeros(acc_refs[h].shape, F32)
    issue_scores(0, buf_a)
    consume(qi, buf_b, own=True)

    def quad(i, carry):
        j = 4 * i
        issue_scores(j + 1, buf_b)
        consume(j, buf_a)
        issue_scores(j + 2, buf_a)
        consume(j + 1, buf_b)
        issue_scores(j + 3, buf_b)
        consume(j + 2, buf_a)
        issue_scores(j + 4, buf_a)
        consume(j + 3, buf_b)
        return carry

    lax.fori_loop(0, qi // 4, quad, 0)
    j0 = (qi // 4) * 4
    rem = qi - j0

    @pl.when(rem >= 1)
    def _():
        issue_scores(j0 + 1, buf_b)
        consume(j0, buf_a)

    @pl.when(rem >= 2)
    def _():
        issue_scores(j0 + 2, buf_a)
        consume(j0 + 1, buf_b)

    @pl.when(rem == 3)
    def _():
        consume(j0 + 2, buf_a)

    out_t = jnp.concatenate(
        [acc_refs[h][:HEAD_DIM, :] / acc_refs[h][HEAD_DIM:HEAD_DIM + 1, :] for h in range(hps)], axis=0)
    o_ref[...] = out_t.T


def _moba_prompt(q, kb, vt, km, batch, seq_len):
    t = q.shape[0]
    nb = seq_len // MOBA_BLOCK
    tq = MOBA_BLOCK
    hps = HEADS_PER_STEP
    gw = hps * HEAD_DIM
    scratch = ([pltpu.VMEM((tq, gw), BF16) for _ in range(hps)]
               + [pltpu.VMEM((nb, tq), F32) for _ in range(hps)]
               + [pltpu.VMEM((1, tq), F32) for _ in range(hps)]
               + [pltpu.VMEM((VT_ROWS, tq), F32) for _ in range(hps)]
               + [pltpu.VMEM((MOBA_BLOCK, tq), F32) for _ in range(2 * hps)])
    return pl.pallas_call(
        _moba_prompt_kernel,
        grid=(batch, N_HEADS // hps, nb),
        in_specs=[
            pl.BlockSpec((tq, gw), lambda b, g, i: (b * nb + i, g)),
            pl.BlockSpec((seq_len, gw), lambda b, g, i: (b, g)),
            pl.BlockSpec((hps, nb, VT_ROWS, MOBA_BLOCK), lambda b, g, i: (g, b, 0, 0)),
            pl.BlockSpec((1, nb, gw), lambda b, g, i: (b, 0, g)),
        ],
        out_specs=pl.BlockSpec((tq, gw), lambda b, g, i: (b * nb + i, g)),
        out_shape=jax.ShapeDtypeStruct((t, ATTN_WIDTH), F32),
        scratch_shapes=scratch,
        compiler_params=_params("parallel", "parallel", "arbitrary"),
        name="moba_prompt",
    )(q, kb, vt, km)


SEQS_PER_STEP = 2


def _moba_sample_kernel(n_pages, pt_ref, q_ref, kn_ref, vn_ref, *rest):
    del pt_ref
    o_ref = rest[-1]
    for s in range(q_ref.shape[0]):
        pages = rest[2 * n_pages * s:2 * n_pages * (s + 1)]
        o_ref[s] = _moba_sample_one(q_ref[s], kn_ref[s], vn_ref[s], pages[:n_pages], pages[n_pages:])


def _moba_sample_one(q, k_new, v_new, k_pages, v_pages):
    n_pages = len(k_pages)
    nq = q.shape[0]
    rows = N_HEADS * nq
    ppb = MOBA_BLOCK // PAGE_SIZE
    nb_past = n_pages // ppb
    nt = (((1,), (1,)), ((), ()))

    q_rep = jnp.concatenate([q] * N_HEADS, axis=0)
    r_i = lax.broadcasted_iota(jnp.int32, (rows, ATTN_WIDTH), 0)
    c_i = lax.broadcasted_iota(jnp.int32, (rows, ATTN_WIDTH), 1)
    head_mask = (c_i // HEAD_DIM) == (r_i // nq)
    qmat = jnp.where(head_mask, q_rep, 0.0)
    qmat_b = qmat.astype(BF16)

    def page_t(ref):
        return ref[0].reshape(ATTN_WIDTH, PAGE_SIZE)

    blk_lane = lax.broadcasted_iota(jnp.int32, (ATTN_WIDTH, nb_past), 1)
    kmean_t = jnp.zeros((ATTN_WIDTH, nb_past), F32)
    for n in range(nb_past):
        tot = page_t(k_pages[n * ppb])
        for pg in range(n * ppb + 1, (n + 1) * ppb):
            tot = tot + page_t(k_pages[pg])
        col = jnp.sum(tot, axis=1, keepdims=True) * (1.0 / MOBA_BLOCK)
        kmean_t = jnp.where(blk_lane == n, col, kmean_t)
    gate = jnp.dot(qmat, kmean_t, precision=HIGHEST, preferred_element_type=F32)
    bias = _select_blocks(gate, nb_past, axis=1)

    s_parts = []
    for pg in range(n_pages):
        sp = jnp.dot(qmat_b, page_t(k_pages[pg]).astype(BF16), preferred_element_type=F32)
        n = pg // ppb
        s_parts.append(sp + bias[:, n:n + 1])
    sn = lax.dot_general(qmat_b, k_new.astype(BF16), nt, preferred_element_type=F32)
    key_i = lax.broadcasted_iota(jnp.int32, (rows, nq), 1)
    qry_i = lax.broadcasted_iota(jnp.int32, (rows, nq), 0) % nq
    sn = jnp.where(key_i <= qry_i, sn, NEG_INF)

    m_el = s_parts[0]
    for sp in s_parts[1:]:
        m_el = jnp.maximum(m_el, sp)
    m = jnp.maximum(jnp.max(m_el, axis=1, keepdims=True), jnp.max(sn, axis=1, keepdims=True))
    pn = jnp.exp(sn - m)
    acc = jnp.dot(pn.astype(BF16), v_new.astype(BF16), preferred_element_type=F32)
    l_el = jnp.zeros((rows, PAGE_SIZE), F32)
    for pg in range(n_pages):
        p = jnp.exp(s_parts[pg] - m)
        l_el = l_el + p
        acc = acc + lax.dot_general(p.astype(BF16), page_t(v_pages[pg]).astype(BF16), nt,
                                    preferred_element_type=F32)
    l = jnp.sum(pn, axis=1, keepdims=True) + jnp.sum(l_el, axis=1, keepdims=True)
    out = jnp.where(head_mask, acc / l, 0.0)
    res = out[0:nq]
    for h in range(1, N_HEADS):
        res = res + out[h * nq:(h + 1) * nq]
    return res


def _moba_sample(q, k_new, v_new, cache_k, cache_v, page_table, layer, n_pool):
    bsz, nq, _ = q.shape
    n_pages = page_table.shape[1]
    base = layer * n_pool
    spp = SEQS_PER_STEP if bsz % SEQS_PER_STEP == 0 else 1
    tok = pl.BlockSpec((spp, nq, ATTN_WIDTH), lambda b, pt: (b, 0, 0))

    def page_spec(s, pg):
        return pl.BlockSpec((1, N_HEADS, HEAD_DIM, PAGE_SIZE),
                            lambda b, pt: (base + pt[b * spp + s, pg], 0, 0, 0))

    page_specs, page_args = [], []
    for s in range(spp):
        page_specs += [page_spec(s, pg) for pg in range(n_pages)] * 2
        page_args += [cache_k] * n_pages + [cache_v] * n_pages
    grid_spec = pltpu.PrefetchScalarGridSpec(
        num_scalar_prefetch=1,
        grid=(bsz // spp,),
        in_specs=[tok, tok, tok] + page_specs,
        out_specs=tok,
    )
    return pl.pallas_call(
        functools.partial(_moba_sample_kernel, n_pages),
        grid_spec=grid_spec,
        out_shape=jax.ShapeDtypeStruct((bsz, nq, ATTN_WIDTH), F32),
        compiler_params=_params("parallel"),
        name="moba_sample",
    )(page_table, q, k_new, v_new, *page_args)


def _s5_discretise(a_re, a_im, log_dt, b_re, b_im):
    dt = jnp.exp(log_dt.astype(F32))[:, None]
    ar, ai = a_re.astype(F32), a_im.astype(F32)
    mag = jnp.exp(ar * dt)
    lr, li = mag * jnp.cos(ai * dt), mag * jnp.sin(ai * dt)
    den = ar * ar + ai * ai
    nr = lr - 1.0
    fr = (nr * ar + li * ai) / den
    fi = (li * ar - nr * ai) / den
    br_, bi_ = b_re.astype(F32), b_im.astype(F32)
    bbr = fr[..., None] * br_ - fi[..., None] * bi_
    bbi = fr[..., None] * bi_ + fi[..., None] * br_
    return ar * dt, ai * dt, lr, li, bbr, bbi


def _s5_prompt_mats(a_re, a_im, log_dt, b_re, b_im, c_re, c_im):
    c = SSM_CHUNK
    g_, p_, cdim = SSM_GROUPS, SSM_STATE, SSM_GROUP_DIM
    adr, adi, _, _, bbr, bbi = _s5_discretise(a_re, a_im, log_dt, b_re, b_im)
    tau = jnp.arange(c + 1, dtype=F32)[:, None, None]
    pmag = jnp.exp(adr[None] * tau)
    pw_r, pw_i = pmag * jnp.cos(adi[None] * tau), pmag * jnp.sin(adi[None] * tau)
    cr, ci = c_re.astype(F32), c_im.astype(F32)
    lb_r = pw_r[..., None] * bbr[None] - pw_i[..., None] * bbi[None]
    lb_i = pw_r[..., None] * bbi[None] + pw_i[..., None] * bbr[None]
    kern = (jnp.einsum('gop,tgpi->tgoi', cr, lb_r[:c], precision=HIGHEST)
            - jnp.einsum('gop,tgpi->tgoi', ci, lb_i[:c], precision=HIGHEST))
    krow = kern.transpose(1, 3, 0, 2).reshape(g_, cdim, c * cdim)
    w_r = lb_r[:c][::-1].transpose(1, 0, 3, 2).reshape(g_, c * cdim, p_)
    w_i = lb_i[:c][::-1].transpose(1, 0, 3, 2).reshape(g_, c * cdim, p_)
    pr, pi = pw_r[1:], pw_i[1:]
    v_r = cr[None] * pr[:, :, None, :] - ci[None] * pi[:, :, None, :]
    v_i = -cr[None] * pi[:, :, None, :] - ci[None] * pr[:, :, None, :]
    v_r = v_r.transpose(1, 3, 0, 2).reshape(g_, p_, c * cdim)
    v_i = v_i.transpose(1, 3, 0, 2).reshape(g_, p_, c * cdim)
    decay = jnp.stack([pw_r[c], pw_i[c]], axis=1)
    return krow, w_r.astype(BF16), w_i.astype(BF16), v_r.astype(BF16), v_i.astype(BF16), decay


def _s5_sample_mats(a_re, a_im, log_dt, b_re, b_im, c_re, c_im):
    g_, p_, cdim = SSM_GROUPS, SSM_STATE, SSM_GROUP_DIM
    _, _, lr, li, bbr, bbi = _s5_discretise(a_re, a_im, log_dt, b_re, b_im)
    eye = jnp.eye(g_, dtype=F32)
    b_bd_r = (eye[:, None, :, None] * bbr.transpose(0, 2, 1)[:, :, None, :]).reshape(g_ * cdim, g_ * p_)
    b_bd_i = (eye[:, None, :, None] * bbi.transpose(0, 2, 1)[:, :, None, :]).reshape(g_ * cdim, g_ * p_)
    cr, ci = c_re.astype(F32), c_im.astype(F32)
    c_bd_r = (eye[:, None, :, None] * cr.transpose(0, 2, 1)[:, :, None, :]).reshape(g_ * p_, g_ * cdim)
    c_bd_i = (eye[:, None, :, None] * ci.transpose(0, 2, 1)[:, :, None, :]).reshape(g_ * p_, g_ * cdim)
    return lr.reshape(1, g_ * p_), li.reshape(1, g_ * p_), b_bd_r, b_bd_i, c_bd_r, c_bd_i


def _s5_prompt_kernel(n_batch, u_ref, krow_ref, wr_ref, wi_ref, vr_ref, vi_ref, dec_ref,
                      y_ref, hr_ref, hi_ref, sr_s, si_s, pr_s, pi_s, toep_s):
    u = u_ref[0]
    krow = krow_ref[0]
    lane = lax.broadcasted_iota(jnp.int32, krow.shape, 1)
    cd = SSM_GROUP_DIM
    for s in range(krow.shape[1] // cd):
        blk = krow if s == 0 else jnp.where(lane >= cd * s, pltpu.roll(krow, cd * s, 1), 0.0)
        toep_s[s * cd:(s + 1) * cd, :] = blk.astype(BF16)
    n_chunks = u.shape[0]
    cpb = n_chunks // n_batch
    sr_s[...] = jnp.dot(u, wr_ref[0], preferred_element_type=F32)
    si_s[...] = jnp.dot(u, wi_ref[0], preferred_element_type=F32)
    dr = dec_ref[0, 0:1, :]
    di = dec_ref[0, 1:2, :]

    def step(kk, carry):
        new = []
        for b in range(n_batch):
            hr, hi = carry[2 * b], carry[2 * b + 1]
            r = b * cpb + kk
            pr_s[pl.ds(r, 1), :] = hr
            pi_s[pl.ds(r, 1), :] = hi
            new.append(dr * hr - di * hi + sr_s[pl.ds(r, 1), :])
            new.append(dr * hi + di * hr + si_s[pl.ds(r, 1), :])
        return tuple(new)

    zero = jnp.zeros((1, SSM_STATE), F32)
    fin = lax.fori_loop(0, cpb, step, (zero,) * (2 * n_batch))
    for b in range(n_batch):
        hr_ref[0, b:b + 1, :] = fin[2 * b]
        hi_ref[0, b:b + 1, :] = fin[2 * b + 1]
    y = jnp.dot(u, toep_s[...], preferred_element_type=F32)
    y = y + jnp.dot(pr_s[...].astype(BF16), vr_ref[0], preferred_element_type=F32)
    y = y + jnp.dot(pi_s[...].astype(BF16), vi_ref[0], preferred_element_type=F32)
    y_ref[0] = y


def _s5_prompt(u_g, mats, n_batch):
    krow, w_r, w_i, v_r, v_i, decay = mats
    g_, n_chunks, cw = u_g.shape
    per_g = lambda *shape: pl.BlockSpec((1,) + shape, lambda g: (g,) + (0,) * len(shape))
    st = jax.ShapeDtypeStruct((g_, n_batch, SSM_STATE), F32)
    return pl.pallas_call(
        functools.partial(_s5_prompt_kernel, n_batch),
        grid=(g_,),
        in_specs=[per_g(n_chunks, cw), per_g(SSM_GROUP_DIM, cw), per_g(cw, SSM_STATE), per_g(cw, SSM_STATE),
                  per_g(SSM_STATE, cw), per_g(SSM_STATE, cw), per_g(2, SSM_STATE)],
        out_specs=(per_g(n_chunks, cw), per_g(n_batch, SSM_STATE), per_g(n_batch, SSM_STATE)),
        out_shape=(jax.ShapeDtypeStruct((g_, n_chunks, cw), F32), st, st),
        scratch_shapes=[pltpu.VMEM((n_chunks, SSM_STATE), F32) for _ in range(4)]
                       + [pltpu.VMEM((cw, cw), BF16)],
        compiler_params=_params("parallel"),
        name="s5_prompt",
    )(u_g, krow, w_r, w_i, v_r, v_i, decay)


def _s5_sample_kernel(u_ref, h0r_ref, h0i_ref, lr_ref, li_ref, br_ref, bi_ref, cr_ref, ci_ref,
                      y_ref, hr_ref, hi_ref):
    n_steps = u_ref.shape[0]
    lr, li = lr_ref[...], li_ref[...]
    hr, hi = h0r_ref[...], h0i_ref[...]
    for t in range(n_steps):
        u = u_ref[t]
        bur = jnp.dot(u, br_ref[...], precision=HIGHEST, preferred_element_type=F32)
        bui = jnp.dot(u, bi_ref[...], precision=HIGHEST, preferred_element_type=F32)
        hr, hi = lr * hr - li * hi + bur, lr * hi + li * hr + bui
        y_ref[t] = (jnp.dot(hr, cr_ref[...], precision=HIGHEST, preferred_element_type=F32)
                    - jnp.dot(hi, ci_ref[...], precision=HIGHEST, preferred_element_type=F32))
    hr_ref[...] = hr
    hi_ref[...] = hi


def _s5_sample(u_t, h0r, h0i, mats):
    n_steps, bsz, _ = u_t.shape
    st = jax.ShapeDtypeStruct(h0r.shape, F32)
    return pl.pallas_call(
        _s5_sample_kernel,
        out_shape=(jax.ShapeDtypeStruct((n_steps, bsz, SSM_WIDTH), F32), st, st),
        compiler_params=pltpu.CompilerParams(vmem_limit_bytes=VMEM_LIMIT),
        name="s5_sample",
    )(u_t, h0r, h0i, *mats)


def _layer_norm(x, g, b):
    mu = jnp.mean(x, axis=-1, keepdims=True)
    xc = x - mu
    var = jnp.mean(xc * xc, axis=-1, keepdims=True)
    return xc * lax.rsqrt(var + LN_EPS) * g + b


def _rms_gain(y, g):
    return y * lax.rsqrt(jnp.mean(y * y, axis=-1, keepdims=True) + RMS_EPS) * g


def _sigmoid(x):
    return 1.0 / (1.0 + jnp.exp(-x))


def _gelu_tanh(x):
    return 0.5 * x * (1.0 + jnp.tanh(math.sqrt(2.0 / math.pi) * (x + 0.044715 * (x * x * x))))


def _route(lg, le):
    tm = lg.shape[0]
    gi = lax.broadcasted_iota(jnp.int32, lg.shape, 1)
    gmax = jnp.max(lg, axis=-1, keepdims=True)
    g_sel = jnp.min(jnp.where(lg == gmax, gi, N_GROUPS), axis=-1, keepdims=True)
    p_sel = 1.0 / jnp.sum(jnp.exp(lg - gmax), axis=-1, keepdims=True)
    ei = lax.broadcasted_iota(jnp.int32, (tm, N_EXPERTS), 1)
    cand = jnp.where((ei // EXPERTS_PER_GROUP) == g_sel, le, -jnp.inf)
    v0 = jnp.max(cand, axis=-1, keepdims=True)
    i0 = jnp.min(jnp.where(cand == v0, ei, N_EXPERTS), axis=-1, keepdims=True)
    cand = jnp.where(ei == i0, -jnp.inf, cand)
    v1 = jnp.max(cand, axis=-1, keepdims=True)
    i1 = jnp.min(jnp.where(cand == v1, ei, N_EXPERTS), axis=-1, keepdims=True)
    e1 = jnp.exp(v1 - v0)
    w0 = p_sel / (1.0 + e1)
    w1 = p_sel * e1 / (1.0 + e1)
    return jnp.where(ei == i0, w0, 0.0) + jnp.where(ei == i1, w1, 0.0)


def _post_mixer_kernel(y_ref, u_ref, a_ref, x_ref, d_ref, wglu_ref, bglu_ref, ga_ref, gs_ref,
                       wout_ref, g1_ref, b1_ref, wr_ref, br_ref, x1_ref, comb_ref):
    tm = x_ref.shape[0]
    sub = min(tm, POST_MIXER_SUB_ROWS)
    for r in range(tm // sub):
        rows = pl.ds(r * sub, sub)
        y = y_ref[rows, :] + d_ref[...] * u_ref[rows, :]
        g = _gelu_tanh(y)
        z = g * _sigmoid(jnp.dot(g.astype(BF16), wglu_ref[...], preferred_element_type=F32) + bglu_ref[...])
        mix_in = jnp.concatenate([_rms_gain(a_ref[rows, :], ga_ref[...]), _rms_gain(z, gs_ref[...])], axis=-1)
        mixed = jnp.dot(mix_in.astype(BF16), wout_ref[...], preferred_element_type=F32)
        x1 = _layer_norm(ALPHA * x_ref[rows, :] + mixed, g1_ref[...], b1_ref[...])
        x1_ref[rows, :] = x1
        logits = jnp.dot(x1, wr_ref[...], precision=HIGHEST, preferred_element_type=F32) + br_ref[...]
        comb_ref[rows, :] = _route(logits[:, :N_GROUPS], logits[:, N_GROUPS:N_GROUPS + N_EXPERTS])


def _post_mixer(y, u, attn, x, lp, tm):
    t = x.shape[0]
    tm = min(tm, t)
    row = lambda i: (i, 0)
    full = lambda a: pl.BlockSpec(a.shape, lambda i: (0, 0))
    weights = [lp['d'], lp['w_glu'], lp['b_glu'], lp['g_attn'], lp['g_ssm'], lp['w_out'],
               lp['ln1_g'], lp['ln1_b'], lp['w_router'], lp['b_router']]
    return pl.pallas_call(
        _post_mixer_kernel,
        grid=(t // tm,),
        in_specs=[pl.BlockSpec((tm, SSM_WIDTH), row), pl.BlockSpec((tm, SSM_WIDTH), row),
                  pl.BlockSpec((tm, ATTN_WIDTH), row), pl.BlockSpec((tm, D_MODEL), row)]
                 + [full(w) for w in weights],
        out_specs=(pl.BlockSpec((tm, D_MODEL), row), pl.BlockSpec((tm, N_EXPERTS), row)),
        out_shape=(jax.ShapeDtypeStruct((t, D_MODEL), F32), jax.ShapeDtypeStruct((t, N_EXPERTS), F32)),
        compiler_params=_params("parallel"),
        name="post_mixer",
    )(y, u, attn, x, *weights)


def _moe_kernel(x_ref, comb_ref, wg_ref, wu_ref, wd_ref, g2_ref, b2_ref, o_ref, acc_ref):
    e = pl.program_id(1)

    @pl.when(e == 0)
    def _():
        acc_ref[...] = jnp.zeros_like(acc_ref)

    xb = x_ref[...].astype(BF16)
    hg = jnp.dot(xb, wg_ref[0].astype(BF16), preferred_element_type=F32)
    hu = jnp.dot(xb, wu_ref[0].astype(BF16), preferred_element_type=F32)
    h = hg * _sigmoid(hg) * hu
    y = jnp.dot(h.astype(BF16), wd_ref[0].astype(BF16), preferred_element_type=F32)
    comb = comb_ref[...]
    ei = lax.broadcasted_iota(jnp.int32, comb.shape, 1)
    w = jnp.sum(jnp.where(ei == e, comb, 0.0), axis=-1, keepdims=True)
    acc_ref[...] += w * y

    @pl.when(e == pl.num_programs(1) - 1)
    def _():
        o_ref[...] = _layer_norm(ALPHA * x_ref[...] + acc_ref[...], g2_ref[...], b2_ref[...])


def _moe(x1, comb, lp, tm):
    t = x1.shape[0]
    layer = lp['layer']
    row = lambda i, e: (i, 0)
    return pl.pallas_call(
        _moe_kernel,
        grid=(t // tm, N_EXPERTS),
        in_specs=[
            pl.BlockSpec((tm, D_MODEL), row),
            pl.BlockSpec((tm, N_EXPERTS), row),
            pl.BlockSpec((None, 1, D_MODEL, D_EXPERT), lambda i, e: (layer, e, 0, 0)),
            pl.BlockSpec((None, 1, D_MODEL, D_EXPERT), lambda i, e: (layer, e, 0, 0)),
            pl.BlockSpec((None, 1, D_EXPERT, D_MODEL), lambda i, e: (layer, e, 0, 0)),
            pl.BlockSpec((1, D_MODEL), lambda i, e: (0, 0)),
            pl.BlockSpec((1, D_MODEL), lambda i, e: (0, 0)),
        ],
        out_specs=pl.BlockSpec((tm, D_MODEL), row),
        out_shape=jax.ShapeDtypeStruct((t, D_MODEL), F32),
        scratch_shapes=[pltpu.VMEM((tm, D_MODEL), F32)],
        compiler_params=_params("parallel", "arbitrary"),
        name="moe",
    )(x1, comb, lp['w_gate'], lp['w_up'], lp['w_down'], lp['ln2_g'], lp['ln2_b'])


def _prompt_layer(x, lp, batch, seq_len, cos_t, sin_t, layer, depth, kbuf, vbuf):
    t = x.shape[0]
    nb = seq_len // MOBA_BLOCK
    kbuf, vbuf, u, q, kb, vt, km = _inproj_prompt(x, lp['w_in'], cos_t, sin_t, seq_len,
                                                   layer, depth, kbuf, vbuf)
    attn = _moba_prompt(q, kb, vt, km.reshape(batch, nb, ATTN_WIDTH), batch, seq_len)
    c = SSM_CHUNK
    u_g = (u.astype(BF16).reshape(t // c, c, SSM_GROUPS, SSM_GROUP_DIM)
           .transpose(2, 0, 1, 3).reshape(SSM_GROUPS, t // c, c * SSM_GROUP_DIM))
    y_g, hr, hi = _s5_prompt(u_g, lp['s5_prompt'], batch)
    y = (y_g.reshape(SSM_GROUPS, t // c, c, SSM_GROUP_DIM)
         .transpose(1, 2, 0, 3).reshape(t, SSM_WIDTH))
    x1, comb = _post_mixer(y, u, attn, x, lp, tm=512)
    x2 = _moe(x1, comb, lp, tm=1024)
    return x2, kbuf, vbuf, hr.transpose(1, 0, 2), hi.transpose(1, 0, 2)


def _sample_layer(x, lp, bsz, nq, cos_t, sin_t, cache_k, cache_v, page_table, layer, n_pool,
                  h0r, h0i):
    t = x.shape[0]
    q, k, v, u = _inproj_sample(x, lp['w_in'], cos_t, sin_t)
    to3 = lambda a: a.reshape(bsz, nq, ATTN_WIDTH)
    attn = _moba_sample(to3(q), to3(k), to3(v), cache_k, cache_v, page_table, layer, n_pool)
    attn = attn.reshape(t, ATTN_WIDTH)
    u_t = u.reshape(bsz, nq, SSM_WIDTH).transpose(1, 0, 2)
    y_t, hr, hi = _s5_sample(u_t, h0r, h0i, lp['s5_sample'])
    y = y_t.transpose(1, 0, 2).reshape(t, SSM_WIDTH)
    x1, comb = _post_mixer(y, u, attn, x, lp, tm=256)
    x2 = _moe(x1, comb, lp, tm=t)
    return x2, k, v, hr, hi


def _all_layer_params(w_in, w_out, norm_attn_g, norm_ssm_g, ssm_a_re, ssm_a_im, ssm_log_dt,
                      ssm_b_re, ssm_b_im, ssm_c_re, ssm_c_im, ssm_d, w_glu, b_glu, ln1_g, ln1_b,
                      w_group, b_group, w_expert, b_expert, w_gate, w_up, w_down, ln2_g, ln2_b):
    depth = w_in.shape[0]
    rowv = lambda a: a.reshape(depth, 1, -1).astype(F32)
    s5_args = (ssm_a_re, ssm_a_im, ssm_log_dt, ssm_b_re, ssm_b_im, ssm_c_re, ssm_c_im)
    pad = LANES - N_GROUPS - N_EXPERTS
    w_router = jnp.concatenate([w_group, w_expert, jnp.zeros((depth, D_MODEL, pad), F32)], axis=2)
    b_router = jnp.concatenate([b_group, b_expert, jnp.zeros((depth, pad), F32)], axis=1)
    return dict(
        w_in=w_in.astype(BF16), w_out=w_out.astype(BF16), w_glu=w_glu.astype(BF16),
        g_attn=rowv(norm_attn_g), g_ssm=rowv(norm_ssm_g), d=rowv(ssm_d), b_glu=rowv(b_glu),
        ln1_g=rowv(ln1_g), ln1_b=rowv(ln1_b), ln2_g=rowv(ln2_g), ln2_b=rowv(ln2_b),
        w_router=w_router.astype(F32), b_router=rowv(b_router),
        s5_prompt=jax.vmap(_s5_prompt_mats)(*s5_args), s5_sample=jax.vmap(_s5_sample_mats)(*s5_args),
    ), dict(w_gate=w_gate, w_up=w_up, w_down=w_down)


def _layer_params(stacked, experts, l):
    lp = jax.tree.map(lambda a: a[l], stacked)
    return dict(lp, layer=l, **experts)


def kernel(x_prompt, x_sample, cache_k, cache_v, state_ssm_re, state_ssm_im, page_table, w_in, w_out, norm_attn_g, norm_ssm_g, ssm_a_re, ssm_a_im, ssm_log_dt, ssm_b_re, ssm_b_im, ssm_c_re, ssm_c_im, ssm_d, w_glu, b_glu, ln1_g, ln1_b, w_group, b_group, w_expert, b_expert, w_gate, w_up, w_down, ln2_g, ln2_b):
    batch, seq_len, _ = x_prompt.shape
    dec_batch, dec_seq, _ = x_sample.shape
    depth, n_pool = cache_k.shape[0], cache_k.shape[1]
    n_pages = page_table.shape[1]
    past_len = n_pages * PAGE_SIZE
    assert seq_len % 512 == 0 and past_len % MOBA_BLOCK == 0 and dec_seq <= MOBA_BLOCK

    cos_p, sin_p = _rope_tables(jnp.arange(seq_len, dtype=jnp.int32))
    pos_s = past_len + jnp.arange(dec_seq, dtype=jnp.int32)
    cos_s, sin_s = _rope_tables(jnp.tile(pos_s, dec_batch))

    to_pages = lambda c: c.transpose(0, 1, 3, 4, 2).reshape(depth * n_pool, N_HEADS, HEAD_DIM, PAGE_SIZE)
    ck, cv = to_pages(cache_k), to_pages(cache_v)
    xp = x_prompt.reshape(batch * seq_len, D_MODEL)
    xs = x_sample.reshape(dec_batch * dec_seq, D_MODEL)
    kbuf = jnp.zeros((depth, batch, N_HEADS, HEAD_DIM, seq_len), F32)
    vbuf = jnp.zeros((depth, batch, N_HEADS, HEAD_DIM, seq_len), F32)
    outs = [[] for _ in range(6)]
    stacked, experts = _all_layer_params(
        w_in, w_out, norm_attn_g, norm_ssm_g, ssm_a_re, ssm_a_im, ssm_log_dt, ssm_b_re, ssm_b_im,
        ssm_c_re, ssm_c_im, ssm_d, w_glu, b_glu, ln1_g, ln1_b, w_group, b_group, w_expert, b_expert,
        w_gate, w_up, w_down, ln2_g, ln2_b)
    for l in range(depth):
        lp = _layer_params(stacked, experts, l)
        xp, kbuf, vbuf, hrp, hip = _prompt_layer(xp, lp, batch, seq_len, cos_p, sin_p, l, depth, kbuf, vbuf)
        h0r = state_ssm_re[l].reshape(dec_batch, SSM_GROUPS * SSM_STATE)
        h0i = state_ssm_im[l].reshape(dec_batch, SSM_GROUPS * SSM_STATE)
        xs, kn, vn, hrs, his = _sample_layer(xs, lp, dec_batch, dec_seq, cos_s, sin_s, ck, cv,
                                             page_table, l, n_pool, h0r, h0i)
        outs[0].append(hrp)
        outs[1].append(hip)
        outs[2].append(kn.reshape(dec_batch, dec_seq, N_HEADS, HEAD_DIM))
        outs[3].append(vn.reshape(dec_batch, dec_seq, N_HEADS, HEAD_DIM))
        outs[4].append(hrs.reshape(dec_batch, SSM_GROUPS, SSM_STATE))
        outs[5].append(his.reshape(dec_batch, SSM_GROUPS, SSM_STATE))
    return (xp.reshape(batch, seq_len, D_MODEL), xs.reshape(dec_batch, dec_seq, D_MODEL),
            kbuf.transpose(0, 1, 4, 2, 3), vbuf.transpose(0, 1, 4, 2, 3), *[jnp.stack(o) for o in outs])
```
